```python
import math
import jax, jax.numpy as jnp
from jax import lax
import numpy as np

D_MODEL = 1024
BATCH = 4
SEQ = 8192
DEPTH = 2

CHUNK = 64
Q_BLOCK = 128
D_MIX = D_MODEL
D_SSM = D_MIX // 2
SSM_GROUP = 16
N_SSM_GROUPS = D_SSM // SSM_GROUP
SSM_STATE = 64
D_ATTN = D_MIX - D_SSM
N_HEADS = 4
HEAD_DIM_V = D_ATTN // N_HEADS
HEAD_DIM_QK = HEAD_DIM_V // 2
D_IN = D_SSM + 3 * D_ATTN
ROPE_THETA = 10000.0
N_EXPERTS = 32
TOP_K = 4
D_FF = D_MODEL
SWIGLU_LIMIT = 7.0
SWIGLU_ALPHA = 1.702
MOE_BLOCK = 128
EPS = 1e-6

kernel_name = "hymba_s5_diffattn_gptoss_moe"


def rms_norm(x, g):
    xf = x.astype(jnp.float32)
    y = xf * lax.rsqrt(jnp.mean(xf * xf, axis=-1, keepdims=True) + EPS)
    return (y * g.astype(jnp.float32)).astype(x.dtype)


def rope_tables(seq_len):
    inv = 1.0 / (ROPE_THETA ** (jnp.arange(0, HEAD_DIM_QK, 2, dtype=jnp.float32) / HEAD_DIM_QK))
    ang = jnp.arange(seq_len, dtype=jnp.float32)[:, None] * inv[None, :]
    return jnp.cos(ang), jnp.sin(ang)


def apply_rope(x, cos, sin):
    c = cos[None, :, None, None, :].astype(x.dtype)
    s = sin[None, :, None, None, :].astype(x.dtype)
    x1, x2 = jnp.split(x, 2, axis=-1)
    return jnp.concatenate([x1 * c - x2 * s, x2 * c + x1 * s], axis=-1)


def _ssm_combine(e1, e2):
    a1, b1 = e1
    a2, b2 = e2
    return a2 * a1, a2 * b1 + b2


def s5_mixer(u, lam_re, lam_im, log_step, b_re, b_im, c_re, c_im, d_skip, w_glu, g_norm):
    bsz, seq, _ = u.shape
    uf = u.astype(jnp.float32).reshape(bsz, seq, N_SSM_GROUPS, SSM_GROUP)
    lam = lax.complex(lam_re.astype(jnp.float32), lam_im.astype(jnp.float32))
    step = jnp.exp(log_step.astype(jnp.float32))[:, None]
    lam_bar = jnp.exp(lam * step)
    b_mat = lax.complex(b_re.astype(jnp.float32), b_im.astype(jnp.float32))
    c_mat = lax.complex(c_re.astype(jnp.float32), c_im.astype(jnp.float32))
    b_bar = ((lam_bar - 1.0) / lam)[..., None] * b_mat
    bu = jnp.einsum('bsgh,gph->bsgp', uf.astype(jnp.complex64), b_bar)
    a = jnp.broadcast_to(lam_bar[None, None], (1, seq, N_SSM_GROUPS, SSM_STATE))
    _, states = lax.associative_scan(_ssm_combine, (a, bu), axis=1)
    y = jnp.einsum('bsgp,ghp->bsgh', states, c_mat).real \
        + d_skip.astype(jnp.float32).reshape(N_SSM_GROUPS, SSM_GROUP) * uf
    y = jax.nn.gelu(y.reshape(bsz, seq, D_SSM))
    y = y * jax.nn.sigmoid(y @ w_glu.astype(jnp.float32))
    return rms_norm(y.astype(u.dtype), g_norm)


def diff_attention(q, k, v, cos, sin, q_norm_g, k_norm_g, lam_q1, lam_k1, lam_q2, lam_k2,
                   subln_g, lambda_init):
    bsz, seq, _ = q.shape
    q = q.reshape(bsz, seq, N_HEADS, 2, HEAD_DIM_QK)
    k = k.reshape(bsz, seq, N_HEADS, 2, HEAD_DIM_QK)
    v = v.reshape(bsz, seq, N_HEADS, HEAD_DIM_V)
    q = apply_rope(rms_norm(q, q_norm_g), cos, sin)
    k = apply_rope(rms_norm(k, k_norm_g), cos, sin)
    lam = (jnp.exp(jnp.sum(lam_q1.astype(jnp.float32) * lam_k1.astype(jnp.float32)))
           - jnp.exp(jnp.sum(lam_q2.astype(jnp.float32) * lam_k2.astype(jnp.float32)))
           + lambda_init)
    scale = 1.0 / math.sqrt(HEAD_DIM_QK)
    n_blk = seq // Q_BLOCK
    k_chunk = jnp.arange(seq) // CHUNK
    q_blocks = jnp.moveaxis(q.reshape(bsz, n_blk, Q_BLOCK, N_HEADS, 2, HEAD_DIM_QK), 1, 0)

    def attend(args):
        qb, bi = args
        s = jnp.einsum('bqhmd,bkhmd->bhmqk', qb, k).astype(jnp.float32) * scale
        q_chunk = (bi * Q_BLOCK + jnp.arange(Q_BLOCK)) // CHUNK
        mask = k_chunk[None, :] <= q_chunk[:, None]
        p = jax.nn.softmax(jnp.where(mask, s, -jnp.inf), axis=-1)
        pd = p[:, :, 0] - lam * p[:, :, 1]
        return jnp.einsum('bhqk,bkhd->bqhd', pd.astype(v.dtype), v)

    o = lax.map(attend, (q_blocks, jnp.arange(n_blk)))
    o = jnp.moveaxis(o, 0, 1).reshape(bsz, seq, N_HEADS, HEAD_DIM_V)
    o = rms_norm(o, subln_g) * (1.0 - lambda_init)
    return o.reshape(bsz, seq, D_ATTN)


def moe_ffn(xn, w_router, b_router, w_up, b_up, w_down, b_down):
    bsz, seq, dm = xn.shape
    n_tok = bsz * seq
    xt = xn.reshape(n_tok, dm)
    logits = (xt @ w_router + b_router).astype(jnp.float32)
    top_val, top_idx = lax.top_k(logits, TOP_K)
    gates = jax.nn.softmax(top_val, axis=-1)
    n_assign = n_tok * TOP_K
    expert = top_idx.reshape(-1)
    token = jnp.arange(n_assign) // TOP_K
    order = jnp.argsort(expert, stable=True)
    e_s, t_s, g_s = expert[order], token[order], gates.reshape(-1)[order]
    counts = jnp.bincount(expert, length=N_EXPERTS)
    padded = (counts + MOE_BLOCK - 1) // MOE_BLOCK * MOE_BLOCK
    pad_end = jnp.cumsum(padded)
    pad_start = pad_end - padded
    start = jnp.cumsum(counts) - counts
    dest = pad_start[e_s] + jnp.arange(n_assign) - start[e_s]
    n_blocks = -(-n_assign // MOE_BLOCK) + N_EXPERTS
    buf = jnp.zeros((n_blocks * MOE_BLOCK, dm), xt.dtype).at[dest].set(xt[t_s])
    block_expert = jnp.minimum(
        jnp.searchsorted(pad_end, jnp.arange(n_blocks) * MOE_BLOCK, side='right'), N_EXPERTS - 1)

    def expert_block(args):
        xb, e = args
        h = xb @ w_up[e] + b_up[e]
        glu, lin = jnp.split(h, 2, axis=-1)
        glu = jnp.minimum(glu, SWIGLU_LIMIT)
        lin = jnp.clip(lin, -SWIGLU_LIMIT, SWIGLU_LIMIT)
        act = glu * jax.nn.sigmoid(SWIGLU_ALPHA * glu) * (lin + 1.0)
        return act @ w_down[e] + b_down[e]

    y_buf = lax.map(expert_block, (buf.reshape(n_blocks, MOE_BLOCK, dm), block_expert))
    y = y_buf.reshape(-1, dm)[dest] * g_s[:, None].astype(y_buf.dtype)
    out = jax.ops.segment_sum(y, t_s, num_segments=n_tok)
    return out.reshape(bsz, seq, dm).astype(xn.dtype)


def setup_inputs(seed: int = 0) -> dict:
    key = jax.random.key(seed)
    ks = jax.random.split(key, 32)

    def nrm(k, shape, scale):
        return jax.random.normal(k, shape, jnp.float32) * scale

    G, P, H = N_SSM_GROUPS, SSM_STATE, SSM_GROUP
    lam_re = -0.5 + 0.01 * jax.random.uniform(ks[3], (DEPTH, G, P), jnp.float32, -1.0, 1.0)
    lam_im = jnp.pi * jnp.arange(P, dtype=jnp.float32)[None, None, :] + nrm(ks[4], (DEPTH, G, P), 0.01)
    log_step = jax.random.uniform(ks[5], (DEPTH, G), jnp.float32, math.log(1e-3), math.log(1e-1))
    return {
        "x": nrm(ks[0], (BATCH, SEQ, D_MODEL), 1.0),
        "g_mix": 1.0 + nrm(ks[1], (DEPTH, D_MODEL), 0.02),
        "w_in": nrm(ks[2], (DEPTH, D_MODEL, D_IN), D_MODEL ** -0.5),
        "ssm_lam_re": lam_re,
        "ssm_lam_im": lam_im,
        "ssm_log_step": log_step,
        "ssm_b_re": nrm(ks[6], (DEPTH, G, P, H), (2.0 * H) ** -0.5),
        "ssm_b_im": nrm(ks[7], (DEPTH, G, P, H), (2.0 * H) ** -0.5),
        "ssm_c_re": nrm(ks[8], (DEPTH, G, H, P), (2.0 * P) ** -0.5),
        "ssm_c_im": nrm(ks[9], (DEPTH, G, H, P), (2.0 * P) ** -0.5),
        "ssm_d": nrm(ks[10], (DEPTH, D_SSM), 1.0),
        "ssm_w_glu": nrm(ks[11], (DEPTH, D_SSM, D_SSM), D_SSM ** -0.5),
        "ssm_norm_g": 1.0 + nrm(ks[12], (DEPTH, D_SSM), 0.02),
        "q_norm_g": 1.0 + nrm(ks[13], (DEPTH, HEAD_DIM_QK), 0.02),
        "k_norm_g": 1.0 + nrm(ks[14], (DEPTH, HEAD_DIM_QK), 0.02),
        "lam_q1": nrm(ks[15], (DEPTH, HEAD_DIM_QK), 0.1),
        "lam_k1": nrm(ks[16], (DEPTH, HEAD_DIM_QK), 0.1),
        "lam_q2": nrm(ks[17], (DEPTH, HEAD_DIM_QK), 0.1),
        "lam_k2": nrm(ks[18], (DEPTH, HEAD_DIM_QK), 0.1),
        "subln_g": 1.0 + nrm(ks[19], (DEPTH, HEAD_DIM_V), 0.02),
        "w_out": nrm(ks[20], (DEPTH, D_MIX, D_MODEL), D_MIX ** -0.5),
        "g_ffn": 1.0 + nrm(ks[21], (DEPTH, D_MODEL), 0.02),
        "w_router": nrm(ks[22], (DEPTH, D_MODEL, N_EXPERTS), D_MODEL ** -0.5),
        "b_router": nrm(ks[23], (DEPTH, N_EXPERTS), 0.01),
        "w_up": nrm(ks[24], (DEPTH, N_EXPERTS, D_MODEL, 2 * D_FF), D_MODEL ** -0.5),
        "b_up": nrm(ks[25], (DEPTH, N_EXPERTS, 2 * D_FF), 0.01),
        "w_down": nrm(ks[26], (DEPTH, N_EXPERTS, D_FF, D_MODEL), D_FF ** -0.5),
        "b_down": nrm(ks[27], (DEPTH, N_EXPERTS, D_MODEL), 0.01),
    }


def reference(x, g_mix, w_in, ssm_lam_re, ssm_lam_im, ssm_log_step, ssm_b_re, ssm_b_im,
              ssm_c_re, ssm_c_im, ssm_d, ssm_w_glu, ssm_norm_g, q_norm_g, k_norm_g,
              lam_q1, lam_k1, lam_q2, lam_k2, subln_g, w_out, g_ffn, w_router, b_router,
              w_up, b_up, w_down, b_down):
    seq = x.shape[1]
    cos, sin = rope_tables(seq)
    h = x
    for l in range(DEPTH):
        lambda_init = 0.8 - 0.6 * math.exp(-0.3 * l)
        xn = rms_norm(h, g_mix[l])
        proj = xn @ w_in[l]
        u = proj[..., :D_SSM]
        q = proj[..., D_SSM:D_SSM + D_ATTN]
        k = proj[..., D_SSM + D_ATTN:D_SSM + 2 * D_ATTN]
        v = proj[..., D_SSM + 2 * D_ATTN:]
        y_ssm = s5_mixer(u, ssm_lam_re[l], ssm_lam_im[l], ssm_log_step[l], ssm_b_re[l], ssm_b_im[l],
                         ssm_c_re[l], ssm_c_im[l], ssm_d[l], ssm_w_glu[l], ssm_norm_g[l])
        y_att = diff_attention(q, k, v, cos, sin, q_norm_g[l], k_norm_g[l], lam_q1[l], lam_k1[l],
                               lam_q2[l], lam_k2[l], subln_g[l], lambda_init)
        h = h + jnp.concatenate([y_ssm, y_att], axis=-1) @ w_out[l]
        h = h + moe_ffn(rms_norm(h, g_ffn[l]), w_router[l], b_router[l], w_up[l], b_up[l],
                        w_down[l], b_down[l])
    return h
```

```python
import functools
import math

import jax
import jax.numpy as jnp
from jax import lax
from jax.experimental import pallas as pl
from jax.experimental.pallas import tpu as pltpu

F32 = jnp.float32
BF16 = jnp.bfloat16
I32 = jnp.int32

D_MODEL = 1024
CHUNK = 64
D_SSM = 512
SSM_GROUP = 16
N_GROUPS = 32
SSM_STATE = 64
D_ATTN = 512
N_HEADS = 4
HEAD_DIM_V = 128
HEAD_DIM_QK = 64
D_IN = D_SSM + 3 * D_ATTN
ROPE_THETA = 10000.0
N_EXPERTS = 32
TOP_K = 4
D_FF = 1024
SWIGLU_LIMIT = 7.0
SWIGLU_ALPHA = 1.702
EPS = 1e-6

LANES = 128
SUBLANES = 8
MXU_TILE = 256
STATE_LANES = 2 * N_GROUPS * SSM_STATE
N_STATE_TILES = STATE_LANES // MXU_TILE
ROW_TILE = D_MODEL // LANES
VMEM_LIMIT = 56 * 1024 * 1024


def _cparams(sem, vmem=VMEM_LIMIT):
    return pltpu.CompilerParams(dimension_semantics=sem, vmem_limit_bytes=vmem)


def _store_token_tiles(ref, val, base=0):
    n = val.shape[0]
    for c in range(ROW_TILE):
        ref[pl.ds(base + c, n, stride=ROW_TILE), :] = val[:, c * LANES:(c + 1) * LANES]


def _load_token_tiles(ref, n, base=0):
    return jnp.concatenate(
        [ref[pl.ds(base + c, n, stride=ROW_TILE), :] for c in range(ROW_TILE)], axis=1)


def _in_proj_body(x_ref, g_ref, w_ref, cos_ref, sin_ref, qg_ref, kg_ref, seg_ref,
                  u_ref, q_ref, k_ref, v_ref):
    x = x_ref[...]
    ms = jnp.mean(x * x, axis=-1, keepdims=True)
    xn = (x * lax.rsqrt(ms + EPS) * g_ref[...]).astype(BF16)
    proj = jnp.dot(xn, w_ref[...], preferred_element_type=F32)
    u_ref[...] = proj[:, :D_SSM].astype(BF16)
    v_ref[...] = proj[:, D_SSM + 2 * D_ATTN:].astype(BF16)

    reps = D_ATTN // LANES
    cos = jnp.concatenate([cos_ref[...]] * reps, axis=1)
    sin = jnp.concatenate([sin_ref[...]] * reps, axis=1)
    lane = lax.broadcasted_iota(I32, cos.shape, 1)
    first_half = (lane & (HEAD_DIM_QK // 2)) == 0

    def prep(z, gain):
        ms = jnp.dot((z * z).astype(BF16), seg_ref[...], preferred_element_type=F32)
        zn = z * lax.rsqrt(ms + EPS) * gain
        half = HEAD_DIM_QK // 2
        swapped = jnp.where(first_half, pltpu.roll(zn, D_ATTN - half, 1), pltpu.roll(zn, half, 1))
        return zn * cos + swapped * sin

    scale = 1.0 / math.sqrt(HEAD_DIM_QK)
    q_ref[...] = (prep(proj[:, D_SSM:D_SSM + D_ATTN], qg_ref[...]) * scale).astype(BF16)
    k_ref[...] = prep(proj[:, D_SSM + D_ATTN:D_SSM + 2 * D_ATTN], kg_ref[...]).astype(BF16)


def _in_proj(x2, g, w_bf, cos_t, sin_t, qg, kg, seg, seq, tm):
    n_tok = x2.shape[0]
    n_pos = seq // tm
    tok = lambda i: (i, 0)
    const = lambda i: (0, 0)
    pos = lambda i: (i % n_pos, 0)
    out = jax.ShapeDtypeStruct((n_tok, D_SSM), BF16)
    return pl.pallas_call(
        _in_proj_body,
        grid=(n_tok // tm,),
        in_specs=[
            pl.BlockSpec((tm, D_MODEL), tok),
            pl.BlockSpec((1, D_MODEL), const),
            pl.BlockSpec((D_MODEL, D_IN), const),
            pl.BlockSpec((tm, LANES), pos),
            pl.BlockSpec((tm, LANES), pos),
            pl.BlockSpec((1, D_ATTN), const),
            pl.BlockSpec((1, D_ATTN), const),
            pl.BlockSpec((D_ATTN, D_ATTN), const),
        ],
        out_specs=[pl.BlockSpec((tm, D_SSM), tok)] * 4,
        out_shape=[out] * 4,
        compiler_params=_cparams(("arbitrary",)),
        name="in_proj",
    )(x2, g, w_bf, cos_t, sin_t, qg, kg, seg)


def _ssm_body(u_ref, bexp_ref, cexp_ref, coef_ref, d_ref, wglu_ref, gn_ref, o_ref,
              bu_ref, y_ref, carry_ref, *, tm):
    @pl.when(pl.program_id(1) == 0)
    def _():
        carry_ref[...] = jnp.zeros_like(carry_ref)

    n_blocks = tm // SUBLANES
    tiles_per_k = MXU_TILE // (2 * SSM_GROUP)
    for j in range(N_STATE_TILES):
        kt = j // tiles_per_k
        lo = j * MXU_TILE
        mid = lo + LANES
        hi = lo + MXU_TILE
        uj = u_ref[:, kt * MXU_TILE:(kt + 1) * MXU_TILE]
        bu_ref[...] = jnp.dot(uj, bexp_ref[j], preferred_element_type=F32)
        cre = [coef_ref[t, :, lo:mid] for t in range(4)]
        cim = [coef_ref[t, :, mid:hi] for t in range(4)]

        def body(r, carry, cre=cre, cim=cim):
            cr, ci = carry
            row = pl.multiple_of(r * SUBLANES, SUBLANES)
            xr = bu_ref[pl.ds(row, SUBLANES), 0:LANES]
            xi = bu_ref[pl.ds(row, SUBLANES), LANES:MXU_TILE]
            for t, shift in enumerate((1, 2, 4)):
                sr = pltpu.roll(xr, shift, 0)
                si = pltpu.roll(xi, shift, 0)
                xr, xi = (xr + (cre[t] * sr - cim[t] * si),
                          xi + (cre[t] * si + cim[t] * sr))
            xr, xi = (xr + (cre[3] * cr - cim[3] * ci),
                      xi + (cre[3] * ci + cim[3] * cr))
            bu_ref[pl.ds(row, SUBLANES), 0:LANES] = xr
            bu_ref[pl.ds(row, SUBLANES), LANES:MXU_TILE] = xi
            last = SUBLANES - 1
            return (jnp.broadcast_to(xr[last:last + 1, :], xr.shape),
                    jnp.broadcast_to(xi[last:last + 1, :], xi.shape))

        cr, ci = lax.fori_loop(0, n_blocks, body,
                               (carry_ref[:, lo:mid], carry_ref[:, mid:hi]), unroll=2)
        carry_ref[:, lo:mid] = cr
        carry_ref[:, mid:hi] = ci

        part = jnp.dot(bu_ref[...].astype(BF16), cexp_ref[j], preferred_element_type=F32)
        n0 = kt * MXU_TILE
        if j % tiles_per_k == 0:
            y_ref[:, n0:n0 + MXU_TILE] = part
        else:
            y_ref[:, n0:n0 + MXU_TILE] += part

    y = y_ref[...] + d_ref[...] * u_ref[...].astype(F32)
    y = jax.nn.gelu(y)
    y = y * jax.nn.sigmoid(jnp.dot(y.astype(BF16), wglu_ref[...], preferred_element_type=F32))
    ms = jnp.mean(y * y, axis=-1, keepdims=True)
    o_ref[...] = (y * lax.rsqrt(ms + EPS) * gn_ref[...]).astype(BF16)


def _ssm(u, bexp, cexp, coef, d, wglu_bf, gn, bsz, seq, tm):
    n_s = seq // tm
    tok = lambda b, s: (b * n_s + s, 0)
    c2 = lambda b, s: (0, 0)
    c3 = lambda b, s: (0, 0, 0)
    return pl.pallas_call(
        functools.partial(_ssm_body, tm=tm),
        grid=(bsz, n_s),
        in_specs=[
            pl.BlockSpec((tm, D_SSM), tok),
            pl.BlockSpec((N_STATE_TILES, MXU_TILE, MXU_TILE), c3),
            pl.BlockSpec((N_STATE_TILES, MXU_TILE, MXU_TILE), c3),
            pl.BlockSpec((4, SUBLANES, STATE_LANES), c3),
            pl.BlockSpec((1, D_SSM), c2),
            pl.BlockSpec((D_SSM, D_SSM), c2),
            pl.BlockSpec((1, D_SSM), c2),
        ],
        out_specs=pl.BlockSpec((tm, D_SSM), tok),
        out_shape=jax.ShapeDtypeStruct((bsz * seq, D_SSM), BF16),
        scratch_shapes=[
            pltpu.VMEM((tm, MXU_TILE), F32),
            pltpu.VMEM((tm, D_SSM), F32),
            pltpu.VMEM((SUBLANES, STATE_LANES), F32),
        ],
        compiler_params=_cparams(("arbitrary", "arbitrary")),
        name="ssm_mixer",
    )(u, bexp, cexp, coef, d, wglu_bf, gn)


def _ssm_params(lam_re, lam_im, log_step, b_re, b_im, c_re, c_im):
    g_n, p_n = N_GROUPS, SSM_STATE
    step = jnp.exp(log_step.astype(F32))[:, None]
    lam_re = lam_re.astype(F32)
    lam_im = lam_im.astype(F32)
    a = lam_re * step
    b = lam_im * step

    def power(k):
        mag = jnp.exp(k * a)
        return mag * jnp.cos(k * b), mag * jnp.sin(k * b)

    def to_lanes(re, im):
        return jnp.stack([re.reshape(N_STATE_TILES, LANES), im.reshape(N_STATE_TILES, LANES)],
                         axis=1).reshape(STATE_LANES)

    lbr, lbi = power(1.0)
    nr, ni = lbr - 1.0, lbi
    den = lam_re * lam_re + lam_im * lam_im
    fr = (nr * lam_re + ni * lam_im) / den
    fi = (ni * lam_re - nr * lam_im) / den
    b_re = b_re.astype(F32)
    b_im = b_im.astype(F32)
    bbr = fr[..., None] * b_re - fi[..., None] * b_im
    bbi = fr[..., None] * b_im + fi[..., None] * b_re

    eye = jnp.eye(g_n, dtype=F32)
    bb = jnp.stack([bbr, bbi], axis=0)
    dense_b = jnp.einsum('cgph,kg->khcgp', bb, eye)
    dense_b = dense_b.reshape(D_SSM, 2, N_STATE_TILES, 2, p_n).transpose(0, 2, 1, 3, 4)
    dense_b = dense_b.reshape(D_SSM, STATE_LANES)
    cc = jnp.stack([c_re.astype(F32), -c_im.astype(F32)], axis=0)
    dense_c = jnp.einsum('cghp,kg->cgpkh', cc, eye)
    dense_c = dense_c.reshape(2, N_STATE_TILES, 2, p_n, D_SSM).transpose(1, 0, 2, 3, 4)
    dense_c = dense_c.reshape(STATE_LANES, D_SSM)

    tiles_per_k = MXU_TILE // (2 * SSM_GROUP)
    bexp = jnp.stack([
        dense_b[(j // tiles_per_k) * MXU_TILE:(j // tiles_per_k + 1) * MXU_TILE,
                j * MXU_TILE:(j + 1) * MXU_TILE] for j in range(N_STATE_TILES)])
    cexp = jnp.stack([
        dense_c[j * MXU_TILE:(j + 1) * MXU_TILE,
                (j // tiles_per_k) * MXU_TILE:(j // tiles_per_k + 1) * MXU_TILE]
        for j in range(N_STATE_TILES)])

    rows = jnp.arange(SUBLANES)[:, None]
    tabs = []
    for shift in (1, 2, 4):
        v = to_lanes(*power(float(shift)))
        tabs.append(jnp.where(rows >= shift, v[None, :], 0.0))
    tabs.append(jnp.stack([to_lanes(*power(float(i + 1))) for i in range(SUBLANES)]))
    coef = jnp.stack(tabs).astype(F32)
    return bexp.astype(BF16), cexp.astype(BF16), coef


def _attn_body(lam_ref, q_ref, k_ref, v_ref, g_ref, o_ref, m_ref, l_ref, acc_ref,
               *, tq, lambda_init):
    qi = pl.program_id(2)
    q = q_ref[0]
    lane = lax.broadcasted_iota(I32, q.shape, 1)
    zero = jnp.zeros_like(q)
    q_maps = (jnp.where(lane < HEAD_DIM_QK, q, zero), jnp.where(lane >= HEAD_DIM_QK, q, zero))
    m_ref[...] = jnp.full(m_ref.shape, -jnp.inf, F32)
    l_ref[...] = jnp.zeros(l_ref.shape, F32)
    acc_ref[...] = jnp.zeros(acc_ref.shape, F32)

    def step(ki, mask):
        start = pl.multiple_of(ki * tq, tq)
        kb = k_ref[0, pl.ds(start, tq), :]
        vb = v_ref[0, pl.ds(start, tq), :]
        for m in range(2):
            s = lax.dot_general(q_maps[m], kb, (((1,), (1,)), ((), ())),
                                preferred_element_type=F32)
            if mask is not None:
                s = jnp.where(mask, s, -jnp.inf)
            m_old = m_ref[m]
            m_new = jnp.maximum(m_old, jnp.max(s, axis=-1, keepdims=True))
            p = jnp.exp(s - m_new)
            alpha = jnp.exp(m_old - m_new)
            l_ref[m] = alpha * l_ref[m] + jnp.sum(p, axis=-1, keepdims=True)
            acc_ref[m] = alpha * acc_ref[m] + jnp.dot(p.astype(BF16), vb,
                                                      preferred_element_type=F32)
            m_ref[m] = m_new

    def full_block(ki, c):
        step(ki, None)
        return c

    lax.fori_loop(0, qi, full_block, 0)
    rows = lax.broadcasted_iota(I32, (tq, tq), 0) // CHUNK
    cols = lax.broadcasted_iota(I32, (tq, tq), 1) // CHUNK
    step(qi, cols <= rows)

    o = acc_ref[0] / l_ref[0] - lam_ref[0] * (acc_ref[1] / l_ref[1])
    ms = jnp.mean(o * o, axis=-1, keepdims=True)
    o_ref[0] = (o * lax.rsqrt(ms + EPS) * g_ref[...] * (1.0 - lambda_init)).astype(BF16)


def _attention(lam, q, k, v, subln_g, bsz, seq, tq, lambda_init):
    q3 = q.reshape(bsz, seq, D_ATTN)
    k3 = k.reshape(bsz, seq, D_ATTN)
    v3 = v.reshape(bsz, seq, D_ATTN)
    qmap = lambda b, h, i: (b, i, h)
    kvmap = lambda b, h, i: (b, 0, h)
    out = pl.pallas_call(
        functools.partial(_attn_body, tq=tq, lambda_init=lambda_init),
        grid=(bsz, N_HEADS, seq // tq),
        in_specs=[
            pl.BlockSpec(memory_space=pltpu.SMEM),
            pl.BlockSpec((1, tq, HEAD_DIM_V), qmap),
            pl.BlockSpec((1, seq, HEAD_DIM_V), kvmap),
            pl.BlockSpec((1, seq, HEAD_DIM_V), kvmap),
            pl.BlockSpec((1, HEAD_DIM_V), lambda b, h, i: (0, 0)),
        ],
        out_specs=pl.BlockSpec((1, tq, HEAD_DIM_V), qmap),
        out_shape=jax.ShapeDtypeStruct((bsz, seq, D_ATTN), BF16),
        scratch_shapes=[
            pltpu.VMEM((2, tq, 1), F32),
            pltpu.VMEM((2, tq, 1), F32),
            pltpu.VMEM((2, tq, HEAD_DIM_V), F32),
        ],
        compiler_params=_cparams(("arbitrary", "arbitrary", "arbitrary")),
        name="diff_attention",
    )(lam, q3, k3, v3, subln_g)
    return out.reshape(bsz * seq, D_ATTN)


def _out_router_body(h_ref, ys_ref, ya_ref, wo_ref, g_ref, wr_hi_ref, wr_lo_ref, br_ref,
                     h1_ref, xp_ref, idx_ref, gate_ref, rank_ref, cnt_ref, base_ref, *, tm):
    @pl.when(pl.program_id(0) == 0)
    def _():
        base_ref[...] = jnp.zeros_like(base_ref)

    h1 = (h_ref[...]
          + jnp.dot(ys_ref[...], wo_ref[:D_SSM, :], preferred_element_type=F32)
          + jnp.dot(ya_ref[...], wo_ref[D_SSM:, :], preferred_element_type=F32))
    h1_ref[...] = h1
    ms = jnp.mean(h1 * h1, axis=-1, keepdims=True)
    xn = h1 * lax.rsqrt(ms + EPS) * g_ref[...]
    _store_token_tiles(xp_ref, xn)

    x_hi = xn.astype(BF16)
    x_lo = (xn - x_hi.astype(F32)).astype(BF16)
    nt = (((1,), (1,)), ((), ()))
    logits = (lax.dot_general(wr_hi_ref[...], x_hi, nt, preferred_element_type=F32)
              + lax.dot_general(wr_hi_ref[...], x_lo, nt, preferred_element_type=F32)
              + lax.dot_general(wr_lo_ref[...], x_hi, nt, preferred_element_type=F32)
              + br_ref[...])

    e_iota = lax.broadcasted_iota(I32, logits.shape, 0)
    vals = logits
    top_v, top_i = [], []
    for _ in range(TOP_K):
        mx = jnp.max(vals, axis=0, keepdims=True)
        idx = jnp.min(jnp.where(vals == mx, e_iota, N_EXPERTS), axis=0, keepdims=True)
        top_v.append(mx)
        top_i.append(idx)
        vals = jnp.where(e_iota == idx, -jnp.inf, vals)
    ex = [jnp.exp(v - top_v[0]) for v in top_v]
    den = ex[0] + ex[1] + ex[2] + ex[3]
    gate_ref[...] = jnp.concatenate([e / den for e in ex], axis=0)
    idx_ref[...] = jnp.concatenate(top_i, axis=0)

    onehots = [e_iota == i for i in top_i]
    cnt = jnp.zeros(logits.shape, F32)
    for oh in onehots:
        cnt = cnt + oh.astype(F32)
    tri = (lax.broadcasted_iota(I32, (tm, tm), 0)
           <= lax.broadcasted_iota(I32, (tm, tm), 1)).astype(BF16)
    incl = jnp.dot(cnt.astype(BF16), tri, preferred_element_type=F32)
    base = base_ref[:, 0:1]
    excl = incl - cnt + base
    ranks = [jnp.sum(jnp.where(oh, excl, 0.0), axis=0, keepdims=True) for oh in onehots]
    rank_ref[...] = jnp.concatenate(ranks, axis=0).astype(I32)
    new_base = base + incl[:, tm - 1:tm]
    base_ref[...] = jnp.broadcast_to(new_base, base_ref.shape)
    cnt_ref[...] = jnp.broadcast_to(new_base, cnt_ref.shape)


def _out_router(h2, ys, ya, wo_bf, g, wr_hi, wr_lo, br, tm):
    n_tok = h2.shape[0]
    tok = lambda i: (i, 0)
    col = lambda i: (0, i)
    const = lambda i: (0, 0)
    return pl.pallas_call(
        functools.partial(_out_router_body, tm=tm),
        grid=(n_tok // tm,),
        in_specs=[
            pl.BlockSpec((tm, D_MODEL), tok),
            pl.BlockSpec((tm, D_SSM), tok),
            pl.BlockSpec((tm, D_ATTN), tok),
            pl.BlockSpec((D_MODEL, D_MODEL), const),
            pl.BlockSpec((1, D_MODEL), const),
            pl.BlockSpec((N_EXPERTS, D_MODEL), const),
            pl.BlockSpec((N_EXPERTS, D_MODEL), const),
            pl.BlockSpec((N_EXPERTS, 1), const),
        ],
        out_specs=[
            pl.BlockSpec((tm, D_MODEL), tok),
            pl.BlockSpec((tm * ROW_TILE, LANES), tok),
            pl.BlockSpec((TOP_K, tm), col),
            pl.BlockSpec((TOP_K, tm), col),
            pl.BlockSpec((TOP_K, tm), col),
            pl.BlockSpec((N_EXPERTS, LANES), const),
        ],
        out_shape=[
            jax.ShapeDtypeStruct((n_tok, D_MODEL), F32),
            jax.ShapeDtypeStruct((n_tok * ROW_TILE, LANES), F32),
            jax.ShapeDtypeStruct((TOP_K, n_tok), I32),
            jax.ShapeDtypeStruct((TOP_K, n_tok), F32),
            jax.ShapeDtypeStruct((TOP_K, n_tok), I32),
            jax.ShapeDtypeStruct((N_EXPERTS, LANES), F32),
        ],
        scratch_shapes=[pltpu.VMEM((N_EXPERTS, LANES), F32)],
        compiler_params=_cparams(("arbitrary",)),
        name="out_proj_router",
    )(h2, ys, ya, wo_bf, g, wr_hi, wr_lo, br)


def _row_tile(ref, r):
    return ref.at[pl.ds(pl.multiple_of(r * ROW_TILE, ROW_TILE), ROW_TILE), :]


def _dispatch_body(zs_ref, zf_ref, dest_ref, x_ref, xs_hbm, zero_ref, sem, zsem, *, tm, rows):
    @pl.when(pl.program_id(0) == 0)
    def _():
        zero_ref[...] = jnp.zeros_like(zero_ref)

        def fill(e):
            start = pl.multiple_of(zs_ref[e] * ROW_TILE, ROW_TILE)
            return pltpu.make_async_copy(zero_ref, xs_hbm.at[pl.ds(start, rows * ROW_TILE), :], zsem)

        for e in range(2 * N_EXPERTS):
            @pl.when(zf_ref[e] > 0)
            def _():
                fill(e).start()
        for e in range(2 * N_EXPERTS):
            @pl.when(zf_ref[e] > 0)
            def _():
                fill(e).wait()

    def issue(t, c):
        for k in range(TOP_K):
            d = dest_ref[0, 0, k * tm + t]
            pltpu.make_async_copy(_row_tile(x_ref, t), _row_tile(xs_hbm, d), sem).start()
        return c

    lax.fori_loop(0, tm, issue, 0, unroll=8)
    for k in range(TOP_K):
        pltpu.make_async_copy(x_ref, xs_hbm.at[pl.ds(0, tm * ROW_TILE), :], sem).wait()


def _dispatch(zstart, zflag, dest3, xp, n_rows, tm, rows):
    n_tok = xp.shape[0] // ROW_TILE
    return pl.pallas_call(
        functools.partial(_dispatch_body, tm=tm, rows=rows),
        grid_spec=pltpu.PrefetchScalarGridSpec(
            num_scalar_prefetch=2,
            grid=(n_tok // tm,),
            in_specs=[
                pl.BlockSpec((1, 1, TOP_K * tm), lambda i, zs, zf: (i, 0, 0),
                             memory_space=pltpu.SMEM),
                pl.BlockSpec((tm * ROW_TILE, LANES), lambda i, zs, zf: (i, 0)),
            ],
            out_specs=pl.BlockSpec(memory_space=pl.ANY),
            scratch_shapes=[
                pltpu.VMEM((rows * ROW_TILE, LANES), F32),
                pltpu.SemaphoreType.DMA,
                pltpu.SemaphoreType.DMA,
            ],
        ),
        out_shape=jax.ShapeDtypeStruct((n_rows * ROW_TILE, LANES), F32),
        compiler_params=_cparams(("arbitrary",)),
        name="moe_dispatch",
    )(zstart, zflag, dest3, xp)


def _expert_body(be_ref, nu_ref, xs_ref, wu_ref, bu_ref, wd_ref, bd_ref, y_ref,
                 wu_bf, wd_bf, *, rows):
    i = pl.program_id(0)
    prev = be_ref[jnp.maximum(i - 1, 0)]

    @pl.when((i == 0) | (be_ref[i] != prev))
    def _():
        wu_bf[...] = wu_ref[0].astype(BF16)
        wd_bf[...] = wd_ref[0].astype(BF16)

    @pl.when(i < nu_ref[0])
    def _():
        x = _load_token_tiles(xs_ref, rows).astype(BF16)
        h = jnp.dot(x, wu_bf[...], preferred_element_type=F32) + bu_ref[0]
        glu = jnp.minimum(h[:, :D_FF], SWIGLU_LIMIT)
        lin = jnp.clip(h[:, D_FF:], -SWIGLU_LIMIT, SWIGLU_LIMIT)
        act = glu * jax.nn.sigmoid(SWIGLU_ALPHA * glu) * (lin + 1.0)
        y = jnp.dot(act.astype(BF16), wd_bf[...], preferred_element_type=F32) + bd_ref[0]
        _store_token_tiles(y_ref, y)

    @pl.when(i >= nu_ref[0])
    def _():
        y_ref[...] = jnp.zeros_like(y_ref)


def _experts(block_expert, n_used, xs, w_up, b_up, w_down, b_down, rows):
    n_blocks = xs.shape[0] // (rows * ROW_TILE)
    live = lambda i, be, nu: (jnp.minimum(i, nu[0] - 1), 0)
    wmap = lambda i, be, nu: (be[i], 0, 0)
    return pl.pallas_call(
        functools.partial(_expert_body, rows=rows),
        grid_spec=pltpu.PrefetchScalarGridSpec(
            num_scalar_prefetch=2,
            grid=(n_blocks,),
            in_specs=[
                pl.BlockSpec((rows * ROW_TILE, LANES), live),
                pl.BlockSpec((1, D_MODEL, 2 * D_FF), wmap),
                pl.BlockSpec((1, 1, 2 * D_FF), wmap),
                pl.BlockSpec((1, D_FF, D_MODEL), wmap),
                pl.BlockSpec((1, 1, D_MODEL), wmap),
            ],
            out_specs=pl.BlockSpec((rows * ROW_TILE, LANES), lambda i, be, nu: (i, 0)),
            scratch_shapes=[
                pltpu.VMEM((D_MODEL, 2 * D_FF), BF16),
                pltpu.VMEM((D_FF, D_MODEL), BF16),
            ],
        ),
        out_shape=jax.ShapeDtypeStruct(xs.shape, F32),
        compiler_params=_cparams(("arbitrary",)),
        name="moe_experts",
    )(block_expert, n_used, xs, w_up, b_up, w_down, b_down)


def _combine_body(dest_ref, h1_ref, gate_ref, y_hbm, o_ref, buf_ref, sem, *, tm):
    def issue(t, c):
        for k in range(TOP_K):
            d = dest_ref[0, 0, k * tm + t]
            pltpu.make_async_copy(_row_tile(y_hbm, d), _row_tile(buf_ref, k * tm + t), sem).start()
        return c

    lax.fori_loop(0, tm, issue, 0, unroll=8)
    pltpu.make_async_copy(y_hbm.at[pl.ds(0, TOP_K * tm * ROW_TILE), :], buf_ref, sem).wait()

    acc = h1_ref[...]
    gates = gate_ref[...]
    for k in range(TOP_K):
        acc = acc + gates[:, k:k + 1] * _load_token_tiles(buf_ref, tm, base=k * tm * ROW_TILE)
    o_ref[...] = acc


def _combine(dest3, h1, gates_t, y_buf, tm):
    n_tok = h1.shape[0]
    tok = lambda i: (i, 0)
    return pl.pallas_call(
        functools.partial(_combine_body, tm=tm),
        grid=(n_tok // tm,),
        in_specs=[
            pl.BlockSpec((1, 1, TOP_K * tm), lambda i: (i, 0, 0), memory_space=pltpu.SMEM),
            pl.BlockSpec((tm, D_MODEL), tok),
            pl.BlockSpec((tm, TOP_K), tok),
            pl.BlockSpec(memory_space=pl.ANY),
        ],
        out_specs=pl.BlockSpec((tm, D_MODEL), tok),
        out_shape=jax.ShapeDtypeStruct((n_tok, D_MODEL), F32),
        scratch_shapes=[
            pltpu.VMEM((TOP_K * tm * ROW_TILE, LANES), F32),
            pltpu.SemaphoreType.DMA,
        ],
        compiler_params=_cparams(("arbitrary",)),
        name="moe_combine",
    )(dest3, h1, gates_t, y_buf)


def _tiles(seq, n_tok):
    tm = min(512, seq)
    tq = min(256, seq)
    tr = min(256, seq)
    rows = min(512, n_tok // 8)
    return tm, tq, tr, rows


def _rope_tables(seq):
    inv = 1.0 / (ROPE_THETA ** (jnp.arange(0, HEAD_DIM_QK, 2, dtype=F32) / HEAD_DIM_QK))
    ang = jnp.arange(seq, dtype=F32)[:, None] * inv[None, :]
    cos, sin = jnp.cos(ang), jnp.sin(ang)
    reps = LANES // HEAD_DIM_QK
    cos_t = jnp.concatenate([cos, cos] * reps, axis=1)
    sin_t = jnp.concatenate([-sin, sin] * reps, axis=1)
    return cos_t, sin_t


def kernel(x, g_mix, w_in, ssm_lam_re, ssm_lam_im, ssm_log_step, ssm_b_re, ssm_b_im, ssm_c_re, ssm_c_im, ssm_d, ssm_w_glu, ssm_norm_g, q_norm_g, k_norm_g, lam_q1, lam_k1, lam_q2, lam_k2, subln_g, w_out, g_ffn, w_router, b_router, w_up, b_up, w_down, b_down):
    bsz, seq, _ = x.shape
    n_tok = bsz * seq
    depth = w_in.shape[0]
    tm, tq, tr, rows = _tiles(seq, n_tok)
    n_blocks = (n_tok * TOP_K) // rows + N_EXPERTS
    n_rows = n_blocks * rows

    cos_t, sin_t = _rope_tables(seq)
    seg = jnp.kron(jnp.eye(D_ATTN // HEAD_DIM_QK, dtype=F32),
                   jnp.full((HEAD_DIM_QK, HEAD_DIM_QK), 1.0 / HEAD_DIM_QK, F32)).astype(BF16)

    h = x.reshape(n_tok, D_MODEL).astype(F32)
    for l in range(depth):
        lambda_init = 0.8 - 0.6 * math.exp(-0.3 * l)
        tile_g = lambda g: jnp.tile(g.astype(F32), D_ATTN // HEAD_DIM_QK)[None, :]
        u, q, k, v = _in_proj(h, g_mix[l][None, :].astype(F32), w_in[l].astype(BF16), cos_t, sin_t,
                              tile_g(q_norm_g[l]), tile_g(k_norm_g[l]), seg, seq, tm)

        bexp, cexp, coef = _ssm_params(ssm_lam_re[l], ssm_lam_im[l], ssm_log_step[l],
                                       ssm_b_re[l], ssm_b_im[l], ssm_c_re[l], ssm_c_im[l])
        y_ssm = _ssm(u, bexp, cexp, coef, ssm_d[l][None, :].astype(F32),
                     ssm_w_glu[l].astype(BF16), ssm_norm_g[l][None, :].astype(F32), bsz, seq, tm)

        lam = (jnp.exp(jnp.sum(lam_q1[l].astype(F32) * lam_k1[l].astype(F32)))
               - jnp.exp(jnp.sum(lam_q2[l].astype(F32) * lam_k2[l].astype(F32)))
               + lambda_init).reshape(1).astype(F32)
        y_att = _attention(lam, q, k, v, subln_g[l][None, :].astype(F32), bsz, seq, tq, lambda_init)

        wr_t = w_router[l].astype(F32).T
        wr_hi = wr_t.astype(BF16)
        wr_lo = (wr_t - wr_hi.astype(F32)).astype(BF16)
        h1, xp, top_i, gates, rank, cnt = _out_router(
            h, y_ssm, y_att, w_out[l].astype(BF16), g_ffn[l][None, :].astype(F32),
            wr_hi, wr_lo, b_router[l].astype(F32)[:, None], tr)

        counts = cnt[:, 0].astype(I32)
        padded = (counts + rows - 1) // rows * rows
        pad_end = jnp.cumsum(padded)
        pad_start = pad_end - padded
        dest = pad_start[top_i] + rank
        dest3 = dest.reshape(TOP_K, n_tok // tr, tr).transpose(1, 0, 2).reshape(n_tok // tr, 1, TOP_K * tr)
        block_expert = jnp.minimum(
            jnp.searchsorted(pad_end, jnp.arange(n_blocks, dtype=I32) * rows, side='right'),
            N_EXPERTS - 1).astype(I32)
        n_used = (pad_end[-1] // rows).reshape(1).astype(I32)
        spare = n_blocks - 1 - jnp.arange(N_EXPERTS, dtype=I32)
        zstart = jnp.concatenate([jnp.maximum(pad_end - rows, 0), spare * rows]).astype(I32)
        zflag = jnp.concatenate([padded > 0, spare >= n_used[0]]).astype(I32)

        xs = _dispatch(zstart, zflag, dest3, xp, n_rows, tr, rows)
        y_buf = _experts(block_expert, n_used, xs, w_up[l], b_up[l][:, None, :],
                         w_down[l], b_down[l][:, None, :], rows)
        h = _combine(dest3, h1, gates.T, y_buf, tr)
    return h.reshape(bsz, seq, D_MODEL).astype(x.dtype)
```

```python
import functools
import math

import jax
import jax.numpy as jnp
from jax import lax
from jax.experimental import pallas as pl
from jax.experimental.pallas import tpu as pltpu

F32 = jnp.float32
BF16 = jnp.bfloat16
I32 = jnp.int32

D_MODEL = 1024
CHUNK = 64
D_SSM = 512
SSM_GROUP = 16
N_GROUPS = 32
SSM_STATE = 64
D_ATTN = 512
N_HEADS = 4
HEAD_DIM_V = 128
HEAD_DIM_QK = 64
D_IN = D_SSM + 3 * D_ATTN
ROPE_THETA = 10000.0
N_EXPERTS = 32
TOP_K = 4
D_FF = 1024
SWIGLU_LIMIT = 7.0
SWIGLU_ALPHA = 1.702
EPS = 1e-6

LANES = 128
SUBLANES = 8
MXU_TILE = 256
STATE_LANES = 2 * N_GROUPS * SSM_STATE
N_STATE_TILES = STATE_LANES // MXU_TILE
ROW_TILE = D_MODEL // LANES
VMEM_LIMIT = 56 * 1024 * 1024


def _cparams(sem, vmem=VMEM_LIMIT):
    return pltpu.CompilerParams(dimension_semantics=sem, vmem_limit_bytes=vmem)


def _store_token_tiles(ref, val, base=0):
    n = val.shape[0]
    for c in range(ROW_TILE):
        ref[pl.ds(base + c, n, stride=ROW_TILE), :] = val[:, c * LANES:(c + 1) * LANES]


def _load_token_tiles(ref, n, base=0):
    return jnp.concatenate(
        [ref[pl.ds(base + c, n, stride=ROW_TILE), :] for c in range(ROW_TILE)], axis=1)


def _in_proj_body(x_ref, g_ref, w_ref, wvt_ref, cos_ref, sin_ref, qg_ref, kg_ref, seg_ref,
                  u_ref, q_ref, k_ref, vt_ref):
    x = x_ref[...]
    ms = jnp.mean(x * x, axis=-1, keepdims=True)
    xn = (x * lax.rsqrt(ms + EPS) * g_ref[...]).astype(BF16)
    proj = jnp.dot(xn, w_ref[...], preferred_element_type=F32)
    u_ref[...] = proj[:, :D_SSM].astype(BF16)
    vt_ref[...] = lax.dot_general(wvt_ref[...], xn, (((1,), (1,)), ((), ())),
                                  preferred_element_type=F32).astype(BF16)

    reps = D_ATTN // LANES
    cos = jnp.concatenate([cos_ref[...]] * reps, axis=1)
    sin = jnp.concatenate([sin_ref[...]] * reps, axis=1)
    lane = lax.broadcasted_iota(I32, cos.shape, 1)
    first_half = (lane & (HEAD_DIM_QK // 2)) == 0

    def prep(z, gain):
        ms = jnp.dot((z * z).astype(BF16), seg_ref[...], preferred_element_type=F32)
        zn = z * lax.rsqrt(ms + EPS) * gain
        half = HEAD_DIM_QK // 2
        swapped = jnp.where(first_half, pltpu.roll(zn, D_ATTN - half, 1), pltpu.roll(zn, half, 1))
        return zn * cos + swapped * sin

    scale = math.log2(math.e) / math.sqrt(HEAD_DIM_QK)
    q_ref[...] = (prep(proj[:, D_SSM:D_SSM + D_ATTN], qg_ref[...]) * scale).astype(BF16)
    k_ref[...] = prep(proj[:, D_SSM + D_ATTN:D_SSM + 2 * D_ATTN], kg_ref[...]).astype(BF16)


def _in_proj(x2, g, w_bf, wvt_bf, cos_t, sin_t, qg, kg, seg, seq, tm):
    n_tok = x2.shape[0]
    n_pos = seq // tm
    tok = lambda i: (i, 0)
    const = lambda i: (0, 0)
    pos = lambda i: (i % n_pos, 0)
    out = jax.ShapeDtypeStruct((n_tok, D_SSM), BF16)
    out_t = jax.ShapeDtypeStruct((D_ATTN, n_tok), BF16)
    return pl.pallas_call(
        _in_proj_body,
        grid=(n_tok // tm,),
        in_specs=[
            pl.BlockSpec((tm, D_MODEL), tok),
            pl.BlockSpec((1, D_MODEL), const),
            pl.BlockSpec((D_MODEL, D_IN - D_ATTN), const),
            pl.BlockSpec((D_ATTN, D_MODEL), const),
            pl.BlockSpec((tm, LANES), pos),
            pl.BlockSpec((tm, LANES), pos),
            pl.BlockSpec((1, D_ATTN), const),
            pl.BlockSpec((1, D_ATTN), const),
            pl.BlockSpec((D_ATTN, D_ATTN), const),
        ],
        out_specs=[pl.BlockSpec((tm, D_SSM), tok)] * 3 + [pl.BlockSpec((D_ATTN, tm), lambda i: (0, i))],
        out_shape=[out] * 3 + [out_t],
        compiler_params=_cparams(("arbitrary",)),
        name="in_proj",
    )(x2, g, w_bf, wvt_bf, cos_t, sin_t, qg, kg, seg)


def _ssm_body(u_ref, bexp_ref, cexp_ref, coef_ref, d_ref, wglu_ref, gn_ref, o_ref,
              bu_ref, y_ref, carry_ref, *, tm):
    @pl.when(pl.program_id(1) == 0)
    def _():
        carry_ref[...] = jnp.zeros_like(carry_ref)

    half = STATE_LANES // 2
    chunk = 2 * MXU_TILE
    for kt in range(2):
        uk = u_ref[:, kt * MXU_TILE:(kt + 1) * MXU_TILE]
        for c in range(half // chunk):
            bu_ref[:, kt * half + c * chunk:kt * half + (c + 1) * chunk] = jnp.dot(
                uk, bexp_ref[kt, :, c * chunk:(c + 1) * chunk], preferred_element_type=F32)

    def body(r, c):
        row = pl.multiple_of(r * SUBLANES, SUBLANES)
        for j in range(N_STATE_TILES):
            lo = j * MXU_TILE
            mid = lo + LANES
            hi = lo + MXU_TILE
            xr = bu_ref[pl.ds(row, SUBLANES), lo:mid]
            xi = bu_ref[pl.ds(row, SUBLANES), mid:hi]
            for t, shift in enumerate((1, 2, 4)):
                ar = coef_ref[t, :, lo:mid]
                ai = coef_ref[t, :, mid:hi]
                sr = pltpu.roll(xr, shift, 0)
                si = pltpu.roll(xi, shift, 0)
                xr, xi = xr + (ar * sr - ai * si), xi + (ar * si + ai * sr)
            pr = coef_ref[3, :, lo:mid]
            pi = coef_ref[3, :, mid:hi]
            cr = carry_ref[:, lo:mid]
            ci = carry_ref[:, mid:hi]
            xr, xi = xr + (pr * cr - pi * ci), xi + (pr * ci + pi * cr)
            bu_ref[pl.ds(row, SUBLANES), lo:mid] = xr
            bu_ref[pl.ds(row, SUBLANES), mid:hi] = xi
            last = SUBLANES - 1
            carry_ref[:, lo:mid] = jnp.broadcast_to(xr[last:last + 1, :], xr.shape)
            carry_ref[:, mid:hi] = jnp.broadcast_to(xi[last:last + 1, :], xi.shape)
        return c

    lax.fori_loop(0, tm // SUBLANES, body, 0)

    for n in range(2):
        y_ref[:, n * MXU_TILE:(n + 1) * MXU_TILE] = jnp.dot(
            bu_ref[:, n * half:(n + 1) * half].astype(BF16), cexp_ref[n],
            preferred_element_type=F32)

    y = y_ref[...] + d_ref[...] * u_ref[...].astype(F32)
    y = jax.nn.gelu(y)
    y = y * jax.nn.sigmoid(jnp.dot(y.astype(BF16), wglu_ref[...], preferred_element_type=F32))
    ms = jnp.mean(y * y, axis=-1, keepdims=True)
    o_ref[...] = (y * lax.rsqrt(ms + EPS) * gn_ref[...]).astype(BF16)


def _ssm(u, bexp, cexp, coef, d, wglu_bf, gn, bsz, seq, tm):
    n_s = seq // tm
    tok = lambda b, s: (b * n_s + s, 0)
    c2 = lambda b, s: (0, 0)
    c3 = lambda b, s: (0, 0, 0)
    return pl.pallas_call(
        functools.partial(_ssm_body, tm=tm),
        grid=(bsz, n_s),
        in_specs=[
            pl.BlockSpec((tm, D_SSM), tok),
            pl.BlockSpec((2, MXU_TILE, STATE_LANES // 2), c3),
            pl.BlockSpec((2, STATE_LANES // 2, MXU_TILE), c3),
            pl.BlockSpec((4, SUBLANES, STATE_LANES), c3),
            pl.BlockSpec((1, D_SSM), c2),
            pl.BlockSpec((D_SSM, D_SSM), c2),
            pl.BlockSpec((1, D_SSM), c2),
        ],
        out_specs=pl.BlockSpec((tm, D_SSM), tok),
        out_shape=jax.ShapeDtypeStruct((bsz * seq, D_SSM), BF16),
        scratch_shapes=[
            pltpu.VMEM((tm, STATE_LANES), F32),
            pltpu.VMEM((tm, D_SSM), F32),
            pltpu.VMEM((SUBLANES, STATE_LANES), F32),
        ],
        compiler_params=_cparams(("arbitrary", "arbitrary")),
        name="ssm_mixer",
    )(u, bexp, cexp, coef, d, wglu_bf, gn)


def _ssm_params(lam_re, lam_im, log_step, b_re, b_im, c_re, c_im):
    g_n, p_n = N_GROUPS, SSM_STATE
    step = jnp.exp(log_step.astype(F32))[:, None]
    lam_re = lam_re.astype(F32)
    lam_im = lam_im.astype(F32)
    a = lam_re * step
    b = lam_im * step

    def power(k):
        mag = jnp.exp(k * a)
        return mag * jnp.cos(k * b), mag * jnp.sin(k * b)

    def to_lanes(re, im):
        return jnp.stack([re.reshape(N_STATE_TILES, LANES), im.reshape(N_STATE_TILES, LANES)],
                         axis=1).reshape(STATE_LANES)

    lbr, lbi = power(1.0)
    nr, ni = lbr - 1.0, lbi
    den = lam_re * lam_re + lam_im * lam_im
    fr = (nr * lam_re + ni * lam_im) / den
    fi = (ni * lam_re - nr * lam_im) / den
    b_re = b_re.astype(F32)
    b_im = b_im.astype(F32)
    bbr = fr[..., None] * b_re - fi[..., None] * b_im
    bbi = fr[..., None] * b_im + fi[..., None] * b_re

    eye = jnp.eye(g_n, dtype=F32)
    bb = jnp.stack([bbr, bbi], axis=0)
    dense_b = jnp.einsum('cgph,kg->khcgp', bb, eye)
    dense_b = dense_b.reshape(D_SSM, 2, N_STATE_TILES, 2, p_n).transpose(0, 2, 1, 3, 4)
    dense_b = dense_b.reshape(D_SSM, STATE_LANES)
    cc = jnp.stack([c_re.astype(F32), -c_im.astype(F32)], axis=0)
    dense_c = jnp.einsum('cghp,kg->cgpkh', cc, eye)
    dense_c = dense_c.reshape(2, N_STATE_TILES, 2, p_n, D_SSM).transpose(1, 0, 2, 3, 4)
    dense_c = dense_c.reshape(STATE_LANES, D_SSM)

    half = STATE_LANES // 2
    bexp = jnp.stack([dense_b[kt * MXU_TILE:(kt + 1) * MXU_TILE, kt * half:(kt + 1) * half]
                      for kt in range(2)])
    cexp = jnp.stack([dense_c[kt * half:(kt + 1) * half, kt * MXU_TILE:(kt + 1) * MXU_TILE]
                      for kt in range(2)])

    rows = jnp.arange(SUBLANES)[:, None]
    tabs = []
    for shift in (1, 2, 4):
        v = to_lanes(*power(float(shift)))
        tabs.append(jnp.where(rows >= shift, v[None, :], 0.0))
    tabs.append(jnp.stack([to_lanes(*power(float(i + 1))) for i in range(SUBLANES)]))
    coef = jnp.stack(tabs).astype(F32)
    return bexp.astype(BF16), cexp.astype(BF16), coef


def _attn_body(lam_ref, q_ref, k_ref, vt_ref, g_ref, o_ref, acc_ref, *, tq, lambda_init):
    qi = pl.program_id(2)
    qt = q_ref[0].astype(F32).T.astype(BF16)
    row = lax.broadcasted_iota(I32, qt.shape, 0)
    zero = jnp.zeros_like(qt)
    qt_maps = (jnp.where(row < HEAD_DIM_QK, qt, zero), jnp.where(row >= HEAD_DIM_QK, qt, zero))
    acc_ref[...] = jnp.zeros(acc_ref.shape, F32)

    def step(ki, stats, mask):
        start = pl.multiple_of(ki * tq, tq)
        kb = k_ref[0, pl.ds(start, tq), :]
        vtb = vt_ref[:, pl.ds(start, tq)]
        new_stats = []
        for m in range(2):
            m_old, l_old = stats[m]
            s = jnp.dot(kb, qt_maps[m], preferred_element_type=F32)
            if mask is not None:
                s = jnp.where(mask, s, -jnp.inf)
            m_new = jnp.maximum(m_old, jnp.max(s, axis=0, keepdims=True))
            p = jnp.exp2(s - m_new)
            alpha = jnp.exp2(m_old - m_new)
            l_new = alpha * l_old + jnp.sum(p, axis=0, keepdims=True)
            acc_ref[m] = alpha * acc_ref[m] + jnp.dot(vtb, p.astype(BF16),
                                                      preferred_element_type=F32)
            new_stats.append((m_new, l_new))
        return tuple(new_stats)

    init = tuple((jnp.full((1, tq), -jnp.inf, F32), jnp.zeros((1, tq), F32)) for _ in range(2))
    stats = lax.fori_loop(0, qi, lambda ki, st: step(ki, st, None), init)
    key_chunk = lax.broadcasted_iota(I32, (tq, tq), 0) // CHUNK
    qry_chunk = lax.broadcasted_iota(I32, (tq, tq), 1) // CHUNK
    (_, l0), (_, l1) = step(qi, stats, key_chunk <= qry_chunk)

    ot = acc_ref[0] / l0 - lam_ref[0] * (acc_ref[1] / l1)
    ms = jnp.mean(ot * ot, axis=0, keepdims=True)
    gain = jnp.concatenate([g_ref[...]] * (tq // LANES), axis=1)
    ot = ot * lax.rsqrt(ms + EPS) * gain * (1.0 - lambda_init)
    o_ref[0] = ot.T.astype(BF16)


def _attention(lam, q, k, vt, subln_g, bsz, seq, tq, lambda_init):
    q3 = q.reshape(bsz, seq, D_ATTN)
    k3 = k.reshape(bsz, seq, D_ATTN)
    gain = jnp.broadcast_to(subln_g.astype(F32)[:, None], (HEAD_DIM_V, LANES))
    qmap = lambda b, h, i: (b, i, h)
    out = pl.pallas_call(
        functools.partial(_attn_body, tq=tq, lambda_init=lambda_init),
        grid=(bsz, N_HEADS, seq // tq),
        in_specs=[
            pl.BlockSpec(memory_space=pltpu.SMEM),
            pl.BlockSpec((1, tq, HEAD_DIM_V), qmap),
            pl.BlockSpec((1, seq, HEAD_DIM_V), lambda b, h, i: (b, 0, h)),
            pl.BlockSpec((HEAD_DIM_V, seq), lambda b, h, i: (h, b)),
            pl.BlockSpec((HEAD_DIM_V, LANES), lambda b, h, i: (0, 0)),
        ],
        out_specs=pl.BlockSpec((1, tq, HEAD_DIM_V), qmap),
        out_shape=jax.ShapeDtypeStruct((bsz, seq, D_ATTN), BF16),
        scratch_shapes=[pltpu.VMEM((2, HEAD_DIM_V, tq), F32)],
        compiler_params=_cparams(("arbitrary", "arbitrary", "arbitrary")),
        name="diff_attention",
    )(lam, q3, k3, vt, gain)
    return out.reshape(bsz * seq, D_ATTN)


def _out_router_body(h_ref, ys_ref, ya_ref, wo_ref, g_ref, wr_hi_ref, wr_lo_ref, br_ref,
                     h1_ref, xp_ref, idx_ref, gate_ref, rank_ref, cnt_ref, base_ref, *, tm):
    @pl.when(pl.program_id(0) == 0)
    def _():
        base_ref[...] = jnp.zeros_like(base_ref)

    h1 = (h_ref[...]
          + jnp.dot(ys_ref[...], wo_ref[:D_SSM, :], preferred_element_type=F32)
          + jnp.dot(ya_ref[...], wo_ref[D_SSM:, :], preferred_element_type=F32))
    h1_ref[...] = h1
    ms = jnp.mean(h1 * h1, axis=-1, keepdims=True)
    xn = h1 * lax.rsqrt(ms + EPS) * g_ref[...]
    _store_token_tiles(xp_ref, xn)

    x_hi = xn.astype(BF16)
    x_lo = (xn - x_hi.astype(F32)).astype(BF16)
    nt = (((1,), (1,)), ((), ()))
    logits = (lax.dot_general(wr_hi_ref[...], x_hi, nt, preferred_element_type=F32)
              + lax.dot_general(wr_hi_ref[...], x_lo, nt, preferred_element_type=F32)
              + lax.dot_general(wr_lo_ref[...], x_hi, nt, preferred_element_type=F32)
              + br_ref[...])

    e_iota = lax.broadcasted_iota(I32, logits.shape, 0)
    vals = logits
    top_v, top_i = [], []
    for _ in range(TOP_K):
        mx = jnp.max(vals, axis=0, keepdims=True)
        idx = jnp.min(jnp.where(vals == mx, e_iota, N_EXPERTS), axis=0, keepdims=True)
        top_v.append(mx)
        top_i.append(idx)
        vals = jnp.where(e_iota == idx, -jnp.inf, vals)
    ex = [jnp.exp(v - top_v[0]) for v in top_v]
    den = ex[0] + ex[1] + ex[2] + ex[3]
    gate_ref[...] = jnp.concatenate([e / den for e in ex], axis=0)
    idx_ref[...] = jnp.concatenate(top_i, axis=0)

    onehots = [e_iota == i for i in top_i]
    cnt = jnp.zeros(logits.shape, F32)
    for oh in onehots:
        cnt = cnt + oh.astype(F32)
    tri = (lax.broadcasted_iota(I32, (tm, tm), 0)
           <= lax.broadcasted_iota(I32, (tm, tm), 1)).astype(BF16)
    incl = jnp.dot(cnt.astype(BF16), tri, preferred_element_type=F32)
    base = base_ref[:, 0:1]
    excl = incl - cnt + base
    ranks = [jnp.sum(jnp.where(oh, excl, 0.0), axis=0, keepdims=True) for oh in onehots]
    rank_ref[...] = jnp.concatenate(ranks, axis=0).astype(I32)
    new_base = base + incl[:, tm - 1:tm]
    base_ref[...] = jnp.broadcast_to(new_base, base_ref.shape)
    cnt_ref[...] = jnp.broadcast_to(new_base, cnt_ref.shape)


def _out_router(h2, ys, ya, wo_bf, g, wr_hi, wr_lo, br, tm):
    n_tok = h2.shape[0]
    tok = lambda i: (i, 0)
    col = lambda i: (0, i)
    const = lambda i: (0, 0)
    return pl.pallas_call(
        functools.partial(_out_router_body, tm=tm),
        grid=(n_tok // tm,),
        in_specs=[
            pl.BlockSpec((tm, D_MODEL), tok),
            pl.BlockSpec((tm, D_SSM), tok),
            pl.BlockSpec((tm, D_ATTN), tok),
            pl.BlockSpec((D_MODEL, D_MODEL), const),
            pl.BlockSpec((1, D_MODEL), const),
            pl.BlockSpec((N_EXPERTS, D_MODEL), const),
            pl.BlockSpec((N_EXPERTS, D_MODEL), const),
            pl.BlockSpec((N_EXPERTS, 1), const),
        ],
        out_specs=[
            pl.BlockSpec((tm, D_MODEL), tok),
            pl.BlockSpec((tm * ROW_TILE, LANES), tok),
            pl.BlockSpec((TOP_K, tm), col),
            pl.BlockSpec((TOP_K, tm), col),
            pl.BlockSpec((TOP_K, tm), col),
            pl.BlockSpec((N_EXPERTS, LANES), const),
        ],
        out_shape=[
            jax.ShapeDtypeStruct((n_tok, D_MODEL), F32),
            jax.ShapeDtypeStruct((n_tok * ROW_TILE, LANES), F32),
            jax.ShapeDtypeStruct((TOP_K, n_tok), I32),
            jax.ShapeDtypeStruct((TOP_K, n_tok), F32),
            jax.ShapeDtypeStruct((TOP_K, n_tok), I32),
            jax.ShapeDtypeStruct((N_EXPERTS, LANES), F32),
        ],
        scratch_shapes=[pltpu.VMEM((N_EXPERTS, LANES), F32)],
        compiler_params=_cparams(("arbitrary",)),
        name="out_proj_router",
    )(h2, ys, ya, wo_bf, g, wr_hi, wr_lo, br)


def _row_tile(ref, r):
    return ref.at[pl.ds(pl.multiple_of(r * ROW_TILE, ROW_TILE), ROW_TILE), :]


def _dispatch_body(zs_ref, zf_ref, dest_ref, x_ref, xs_hbm, zero_ref, sem, zsem, *, tm, rows):
    @pl.when(pl.program_id(0) == 0)
    def _():
        zero_ref[...] = jnp.zeros_like(zero_ref)

        def fill(e):
            start = pl.multiple_of(zs_ref[e] * ROW_TILE, ROW_TILE)
            return pltpu.make_async_copy(zero_ref, xs_hbm.at[pl.ds(start, rows * ROW_TILE), :], zsem)

        for e in range(2 * N_EXPERTS):
            @pl.when(zf_ref[e] > 0)
            def _():
                fill(e).start()
        for e in range(2 * N_EXPERTS):
            @pl.when(zf_ref[e] > 0)
            def _():
                fill(e).wait()

    def issue(t, c):
        for k in range(TOP_K):
            d = dest_ref[0, 0, k * tm + t]
            pltpu.make_async_copy(_row_tile(x_ref, t), _row_tile(xs_hbm, d), sem).start(priority=k % 2)
        return c

    lax.fori_loop(0, tm, issue, 0, unroll=8)
    for k in range(TOP_K):
        pltpu.make_async_copy(x_ref, xs_hbm.at[pl.ds(0, tm * ROW_TILE), :], sem).wait()


def _dispatch(zstart, zflag, dest3, xp, n_rows, tm, rows):
    n_tok = xp.shape[0] // ROW_TILE
    return pl.pallas_call(
        functools.partial(_dispatch_body, tm=tm, rows=rows),
        grid_spec=pltpu.PrefetchScalarGridSpec(
            num_scalar_prefetch=2,
            grid=(n_tok // tm,),
            in_specs=[
                pl.BlockSpec((1, 1, TOP_K * tm), lambda i, zs, zf: (i, 0, 0),
                             memory_space=pltpu.SMEM),
                pl.BlockSpec((tm * ROW_TILE, LANES), lambda i, zs, zf: (i, 0)),
            ],
            out_specs=pl.BlockSpec(memory_space=pl.ANY),
            scratch_shapes=[
                pltpu.VMEM((rows * ROW_TILE, LANES), F32),
                pltpu.SemaphoreType.DMA,
                pltpu.SemaphoreType.DMA,
            ],
        ),
        out_shape=jax.ShapeDtypeStruct((n_rows * ROW_TILE, LANES), F32),
        compiler_params=_cparams(("arbitrary",)),
        name="moe_dispatch",
    )(zstart, zflag, dest3, xp)


def _expert_body(be_ref, nu_ref, xs_ref, wu_ref, bu_ref, wd_ref, bd_ref, y_ref,
                 wu_bf, wd_bf, *, rows):
    i = pl.program_id(0)
    prev = be_ref[jnp.maximum(i - 1, 0)]

    @pl.when((i == 0) | (be_ref[i] != prev))
    def _():
        wu_bf[...] = wu_ref[0, 0].astype(BF16)
        wd_bf[...] = wd_ref[0, 0].astype(BF16)

    @pl.when(i < nu_ref[0])
    def _():
        x = _load_token_tiles(xs_ref, rows).astype(BF16)
        h = jnp.dot(x, wu_bf[...], preferred_element_type=F32) + bu_ref[0, 0]
        glu = jnp.minimum(h[:, :D_FF], SWIGLU_LIMIT)
        lin = jnp.clip(h[:, D_FF:], -SWIGLU_LIMIT, SWIGLU_LIMIT)
        act = glu * jax.nn.sigmoid(SWIGLU_ALPHA * glu) * (lin + 1.0)
        y = jnp.dot(act.astype(BF16), wd_bf[...], preferred_element_type=F32) + bd_ref[0, 0]
        _store_token_tiles(y_ref, y)

    @pl.when(i >= nu_ref[0])
    def _():
        y_ref[...] = jnp.zeros_like(y_ref)


def _experts(block_expert, n_used, xs, w_up, b_up, w_down, b_down, layer, rows):
    n_blocks = xs.shape[0] // (rows * ROW_TILE)
    live = lambda i, be, nu: (jnp.minimum(i, nu[0] - 1), 0)
    wmap = lambda i, be, nu: (layer, be[i], 0, 0)
    return pl.pallas_call(
        functools.partial(_expert_body, rows=rows),
        grid_spec=pltpu.PrefetchScalarGridSpec(
            num_scalar_prefetch=2,
            grid=(n_blocks,),
            in_specs=[
                pl.BlockSpec((rows * ROW_TILE, LANES), live),
                pl.BlockSpec((1, 1, D_MODEL, 2 * D_FF), wmap),
                pl.BlockSpec((1, 1, 1, 2 * D_FF), wmap),
                pl.BlockSpec((1, 1, D_FF, D_MODEL), wmap),
                pl.BlockSpec((1, 1, 1, D_MODEL), wmap),
            ],
            out_specs=pl.BlockSpec((rows * ROW_TILE, LANES), lambda i, be, nu: (i, 0)),
            scratch_shapes=[
                pltpu.VMEM((D_MODEL, 2 * D_FF), BF16),
                pltpu.VMEM((D_FF, D_MODEL), BF16),
            ],
        ),
        out_shape=jax.ShapeDtypeStruct(xs.shape, F32),
        compiler_params=_cparams(("arbitrary",)),
        name="moe_experts",
    )(block_expert, n_used, xs, w_up, b_up, w_down, b_down)


def _combine_body(dest_ref, h1_ref, gate_ref, y_hbm, o_ref, buf_ref, sem, *, tm):
    def issue(t, c):
        for k in range(TOP_K):
            d = dest_ref[0, 0, k * tm + t]
            pltpu.make_async_copy(_row_tile(y_hbm, d), _row_tile(buf_ref, k * tm + t), sem).start(priority=k % 2)
        return c

    lax.fori_loop(0, tm, issue, 0, unroll=8)
    pltpu.make_async_copy(y_hbm.at[pl.ds(0, TOP_K * tm * ROW_TILE), :], buf_ref, sem).wait()

    acc = h1_ref[...]
    gates = gate_ref[...]
    for k in range(TOP_K):
        acc = acc + gates[:, k:k + 1] * _load_token_tiles(buf_ref, tm, base=k * tm * ROW_TILE)
    o_ref[...] = acc


def _combine(dest3, h1, gates_t, y_buf, tm):
    n_tok = h1.shape[0]
    tok = lambda i: (i, 0)
    return pl.pallas_call(
        functools.partial(_combine_body, tm=tm),
        grid=(n_tok // tm,),
        in_specs=[
            pl.BlockSpec((1, 1, TOP_K * tm), lambda i: (i, 0, 0), memory_space=pltpu.SMEM),
            pl.BlockSpec((tm, D_MODEL), tok),
            pl.BlockSpec((tm, TOP_K), tok),
            pl.BlockSpec(memory_space=pl.ANY),
        ],
        out_specs=pl.BlockSpec((tm, D_MODEL), tok),
        out_shape=jax.ShapeDtypeStruct((n_tok, D_MODEL), F32),
        scratch_shapes=[
            pltpu.VMEM((TOP_K * tm * ROW_TILE, LANES), F32),
            pltpu.SemaphoreType.DMA,
        ],
        compiler_params=_cparams(("arbitrary",)),
        name="moe_combine",
    )(dest3, h1, gates_t, y_buf)


def _tiles(seq, n_tok):
    tm = min(512, seq)
    tq = min(512, seq)
    tr = min(256, seq)
    rows = min(512, n_tok // 8)
    return tm, tq, tr, rows


def _rope_tables(seq):
    inv = 1.0 / (ROPE_THETA ** (jnp.arange(0, HEAD_DIM_QK, 2, dtype=F32) / HEAD_DIM_QK))
    ang = jnp.arange(seq, dtype=F32)[:, None] * inv[None, :]
    cos, sin = jnp.cos(ang), jnp.sin(ang)
    reps = LANES // HEAD_DIM_QK
    cos_t = jnp.concatenate([cos, cos] * reps, axis=1)
    sin_t = jnp.concatenate([-sin, sin] * reps, axis=1)
    return cos_t, sin_t


def kernel(x, g_mix, w_in, ssm_lam_re, ssm_lam_im, ssm_log_step, ssm_b_re, ssm_b_im, ssm_c_re, ssm_c_im, ssm_d, ssm_w_glu, ssm_norm_g, q_norm_g, k_norm_g, lam_q1, lam_k1, lam_q2, lam_k2, subln_g, w_out, g_ffn, w_router, b_router, w_up, b_up, w_down, b_down):
    bsz, seq, _ = x.shape
    n_tok = bsz * seq
    depth = w_in.shape[0]
    tm, tq, tr, rows = _tiles(seq, n_tok)
    n_blocks = (n_tok * TOP_K) // rows + N_EXPERTS
    n_rows = n_blocks * rows

    cos_t, sin_t = _rope_tables(seq)
    seg = jnp.kron(jnp.eye(D_ATTN // HEAD_DIM_QK, dtype=F32),
                   jnp.full((HEAD_DIM_QK, HEAD_DIM_QK), 1.0 / HEAD_DIM_QK, F32)).astype(BF16)

    h = x.reshape(n_tok, D_MODEL).astype(F32)
    for l in range(depth):
        lambda_init = 0.8 - 0.6 * math.exp(-0.3 * l)
        tile_g = lambda g: jnp.tile(g.astype(F32), D_ATTN // HEAD_DIM_QK)[None, :]
        w_uqk = w_in[l][:, :D_IN - D_ATTN].astype(BF16)
        w_vt = w_in[l][:, D_IN - D_ATTN:].T.astype(BF16)
        u, q, k, vt = _in_proj(h, g_mix[l][None, :].astype(F32), w_uqk, w_vt, cos_t, sin_t,
                               tile_g(q_norm_g[l]), tile_g(k_norm_g[l]), seg, seq, tm)

        bexp, cexp, coef = _ssm_params(ssm_lam_re[l], ssm_lam_im[l], ssm_log_step[l],
                                       ssm_b_re[l], ssm_b_im[l], ssm_c_re[l], ssm_c_im[l])
        y_ssm = _ssm(u, bexp, cexp, coef, ssm_d[l][None, :].astype(F32),
                     ssm_w_glu[l].astype(BF16), ssm_norm_g[l][None, :].astype(F32), bsz, seq, tm)

        lam = (jnp.exp(jnp.sum(lam_q1[l].astype(F32) * lam_k1[l].astype(F32)))
               - jnp.exp(jnp.sum(lam_q2[l].astype(F32) * lam_k2[l].astype(F32)))
               + lambda_init).reshape(1).astype(F32)
        y_att = _attention(lam, q, k, vt, subln_g[l], bsz, seq, tq, lambda_init)

        wr_t = w_router[l].astype(F32).T
        wr_hi = wr_t.astype(BF16)
        wr_lo = (wr_t - wr_hi.astype(F32)).astype(BF16)
        h1, xp, top_i, gates, rank, cnt = _out_router(
            h, y_ssm, y_att, w_out[l].astype(BF16), g_ffn[l][None, :].astype(F32),
            wr_hi, wr_lo, b_router[l].astype(F32)[:, None], tr)

        counts = cnt[:, 0].astype(I32)
        padded = (counts + rows - 1) // rows * rows
        pad_end = jnp.cumsum(padded)
        pad_start = pad_end - padded
        e_ids = jnp.arange(N_EXPERTS, dtype=I32)
        start_of = jnp.sum(jnp.where(top_i[None] == e_ids[:, None, None],
                                     pad_start[:, None, None], 0), axis=0)
        dest = start_of + rank
        dest3 = dest.reshape(TOP_K, n_tok // tr, tr).transpose(1, 0, 2).reshape(n_tok // tr, 1, TOP_K * tr)
        block_row = jnp.arange(n_blocks, dtype=I32) * rows
        block_expert = jnp.minimum(
            jnp.sum((pad_end[None, :] <= block_row[:, None]).astype(I32), axis=1),
            N_EXPERTS - 1).astype(I32)
        n_used = (pad_end[-1] // rows).reshape(1).astype(I32)
        spare = n_blocks - 1 - jnp.arange(N_EXPERTS, dtype=I32)
        zstart = jnp.concatenate([jnp.maximum(pad_end - rows, 0), spare * rows]).astype(I32)
        zflag = jnp.concatenate([padded > 0, spare >= n_used[0]]).astype(I32)

        xs = _dispatch(zstart, zflag, dest3, xp, n_rows, tr, rows)
        y_buf = _experts(block_expert, n_used, xs, w_up, b_up[:, :, None, :],
                         w_down, b_down[:, :, None, :], l, rows)
        h = _combine(dest3, h1, gates.T, y_buf, tr)
    return h.reshape(bsz, seq, D_MODEL).astype(x.dtype)
```

```python
import functools
import math

import jax
import jax.numpy as jnp
from jax import lax
from jax.experimental import pallas as pl
from jax.experimental.pallas import tpu as pltpu

F32 = jnp.float32
BF16 = jnp.bfloat16
I32 = jnp.int32

D_MODEL = 1024
CHUNK = 64
D_SSM = 512
SSM_GROUP = 16
N_GROUPS = 32
SSM_STATE = 64
D_ATTN = 512
N_HEADS = 4
HEAD_DIM_V = 128
HEAD_DIM_QK = 64
D_IN = D_SSM + 3 * D_ATTN
ROPE_THETA = 10000.0
N_EXPERTS = 32
TOP_K = 4
D_FF = 1024
SWIGLU_LIMIT = 7.0
SWIGLU_ALPHA = 1.702
EPS = 1e-6

LANES = 128
SUBLANES = 8
MXU_TILE = 256
STATE_LANES = 2 * N_GROUPS * SSM_STATE
N_STATE_TILES = STATE_LANES // MXU_TILE
ROW_TILE = D_MODEL // LANES
VMEM_LIMIT = 56 * 1024 * 1024


def _cparams(sem, vmem=VMEM_LIMIT):
    return pltpu.CompilerParams(dimension_semantics=sem, vmem_limit_bytes=vmem)


def _store_token_tiles(ref, val, base=0):
    n = val.shape[0]
    for c in range(ROW_TILE):
        ref[pl.ds(base + c, n, stride=ROW_TILE), :] = val[:, c * LANES:(c + 1) * LANES]


def _load_token_tiles(ref, n, base=0):
    return jnp.concatenate(
        [ref[pl.ds(base + c, n, stride=ROW_TILE), :] for c in range(ROW_TILE)], axis=1)


def _in_proj_body(x_ref, g_ref, w_ref, wvt_ref, cos_ref, sin_ref, qg_ref, kg_ref, seg_ref,
                  u_ref, q_ref, k_ref, vt_ref):
    x = x_ref[...]
    ms = jnp.mean(x * x, axis=-1, keepdims=True)
    xn = (x * lax.rsqrt(ms + EPS) * g_ref[...]).astype(BF16)
    proj = jnp.dot(xn, w_ref[...], preferred_element_type=F32)
    u_ref[...] = proj[:, :D_SSM].astype(BF16)
    vt_ref[...] = lax.dot_general(wvt_ref[...], xn, (((1,), (1,)), ((), ())),
                                  preferred_element_type=F32).astype(BF16)

    reps = D_ATTN // LANES
    cos = jnp.concatenate([cos_ref[...]] * reps, axis=1)
    sin = jnp.concatenate([sin_ref[...]] * reps, axis=1)
    lane = lax.broadcasted_iota(I32, cos.shape, 1)
    first_half = (lane & (HEAD_DIM_QK // 2)) == 0

    def prep(z, gain):
        ms = jnp.dot((z * z).astype(BF16), seg_ref[...], preferred_element_type=F32)
        zn = z * lax.rsqrt(ms + EPS) * gain
        half = HEAD_DIM_QK // 2
        swapped = jnp.where(first_half, pltpu.roll(zn, D_ATTN - half, 1), pltpu.roll(zn, half, 1))
        return zn * cos + swapped * sin

    scale = math.log2(math.e) / math.sqrt(HEAD_DIM_QK)
    q_ref[...] = (prep(proj[:, D_SSM:D_SSM + D_ATTN], qg_ref[...]) * scale).astype(BF16)
    k_ref[...] = prep(proj[:, D_SSM + D_ATTN:D_SSM + 2 * D_ATTN], kg_ref[...]).astype(BF16)


def _in_proj(x2, g, w_bf, wvt_bf, cos_t, sin_t, qg, kg, seg, seq, tm):
    n_tok = x2.shape[0]
    n_pos = seq // tm
    tok = lambda i: (i, 0)
    const = lambda i: (0, 0)
    pos = lambda i: (i % n_pos, 0)
    out = jax.ShapeDtypeStruct((n_tok, D_SSM), BF16)
    out_t = jax.ShapeDtypeStruct((D_ATTN, n_tok), BF16)
    return pl.pallas_call(
        _in_proj_body,
        grid=(n_tok // tm,),
        in_specs=[
            pl.BlockSpec((tm, D_MODEL), tok),
            pl.BlockSpec((1, D_MODEL), const),
            pl.BlockSpec((D_MODEL, D_IN - D_ATTN), const),
            pl.BlockSpec((D_ATTN, D_MODEL), const),
            pl.BlockSpec((tm, LANES), pos),
            pl.BlockSpec((tm, LANES), pos),
            pl.BlockSpec((1, D_ATTN), const),
            pl.BlockSpec((1, D_ATTN), const),
            pl.BlockSpec((D_ATTN, D_ATTN), const),
        ],
        out_specs=[pl.BlockSpec((tm, D_SSM), tok)] * 3 + [pl.BlockSpec((D_ATTN, tm), lambda i: (0, i))],
        out_shape=[out] * 3 + [out_t],
        compiler_params=_cparams(("arbitrary",)),
        name="in_proj",
    )(x2, g, w_bf, wvt_bf, cos_t, sin_t, qg, kg, seg)


def _ssm_body(u_ref, bexp_ref, cexp_ref, coef_ref, d_ref, wglu_ref, gn_ref, o_ref,
              bu_ref, y_ref, carry_ref, *, tm):
    @pl.when(pl.program_id(1) == 0)
    def _():
        carry_ref[...] = jnp.zeros_like(carry_ref)

    half = STATE_LANES // 2
    chunk = 2 * MXU_TILE
    for kt in range(2):
        uk = u_ref[:, kt * MXU_TILE:(kt + 1) * MXU_TILE]
        for c in range(half // chunk):
            bu_ref[:, kt * half + c * chunk:kt * half + (c + 1) * chunk] = jnp.dot(
                uk, bexp_ref[kt, :, c * chunk:(c + 1) * chunk], preferred_element_type=F32)

    def body(r, c):
        row = pl.multiple_of(r * SUBLANES, SUBLANES)
        for j in range(N_STATE_TILES):
            lo = j * MXU_TILE
            mid = lo + LANES
            hi = lo + MXU_TILE
            xr = bu_ref[pl.ds(row, SUBLANES), lo:mid]
            xi = bu_ref[pl.ds(row, SUBLANES), mid:hi]
            for t, shift in enumerate((1, 2, 4)):
                ar = coef_ref[t, :, lo:mid]
                ai = coef_ref[t, :, mid:hi]
                sr = pltpu.roll(xr, shift, 0)
                si = pltpu.roll(xi, shift, 0)
                xr, xi = xr + (ar * sr - ai * si), xi + (ar * si + ai * sr)
            pr = coef_ref[3, :, lo:mid]
            pi = coef_ref[3, :, mid:hi]
            cr = carry_ref[:, lo:mid]
            ci = carry_ref[:, mid:hi]
            xr, xi = xr + (pr * cr - pi * ci), xi + (pr * ci + pi * cr)
            bu_ref[pl.ds(row, SUBLANES), lo:mid] = xr
            bu_ref[pl.ds(row, SUBLANES), mid:hi] = xi
            last = SUBLANES - 1
            carry_ref[:, lo:mid] = jnp.broadcast_to(xr[last:last + 1, :], xr.shape)
            carry_ref[:, mid:hi] = jnp.broadcast_to(xi[last:last + 1, :], xi.shape)
        return c

    lax.fori_loop(0, tm // SUBLANES, body, 0)

    for n in range(2):
        y_ref[:, n * MXU_TILE:(n + 1) * MXU_TILE] = jnp.dot(
            bu_ref[:, n * half:(n + 1) * half].astype(BF16), cexp_ref[n],
            preferred_element_type=F32)

    y = y_ref[...] + d_ref[...] * u_ref[...].astype(F32)
    y = jax.nn.gelu(y)
    y = y * jax.nn.sigmoid(jnp.dot(y.astype(BF16), wglu_ref[...], preferred_element_type=F32))
    ms = jnp.mean(y * y, axis=-1, keepdims=True)
    o_ref[...] = (y * lax.rsqrt(ms + EPS) * gn_ref[...]).astype(BF16)


def _ssm(u, bexp, cexp, coef, d, wglu_bf, gn, bsz, seq, tm):
    n_s = seq // tm
    tok = lambda b, s: (b * n_s + s, 0)
    c2 = lambda b, s: (0, 0)
    c3 = lambda b, s: (0, 0, 0)
    return pl.pallas_call(
        functools.partial(_ssm_body, tm=tm),
        grid=(bsz, n_s),
        in_specs=[
            pl.BlockSpec((tm, D_SSM), tok),
            pl.BlockSpec((2, MXU_TILE, STATE_LANES // 2), c3),
            pl.BlockSpec((2, STATE_LANES // 2, MXU_TILE), c3),
            pl.BlockSpec((4, SUBLANES, STATE_LANES), c3),
            pl.BlockSpec((1, D_SSM), c2),
            pl.BlockSpec((D_SSM, D_SSM), c2),
            pl.BlockSpec((1, D_SSM), c2),
        ],
        out_specs=pl.BlockSpec((tm, D_SSM), tok),
        out_shape=jax.ShapeDtypeStruct((bsz * seq, D_SSM), BF16),
        scratch_shapes=[
            pltpu.VMEM((tm, STATE_LANES), F32),
            pltpu.VMEM((tm, D_SSM), F32),
            pltpu.VMEM((SUBLANES, STATE_LANES), F32),
        ],
        compiler_params=_cparams(("arbitrary", "arbitrary")),
        name="ssm_mixer",
    )(u, bexp, cexp, coef, d, wglu_bf, gn)


def _ssm_params(lam_re, lam_im, log_step, b_re, b_im, c_re, c_im):
    g_n, p_n = N_GROUPS, SSM_STATE
    step = jnp.exp(log_step.astype(F32))[:, None]
    lam_re = lam_re.astype(F32)
    lam_im = lam_im.astype(F32)
    a = lam_re * step
    b = lam_im * step

    def power(k):
        mag = jnp.exp(k * a)
        return mag * jnp.cos(k * b), mag * jnp.sin(k * b)

    def to_lanes(re, im):
        return jnp.stack([re.reshape(N_STATE_TILES, LANES), im.reshape(N_STATE_TILES, LANES)],
                         axis=1).reshape(STATE_LANES)

    lbr, lbi = power(1.0)
    nr, ni = lbr - 1.0, lbi
    den = lam_re * lam_re + lam_im * lam_im
    fr = (nr * lam_re + ni * lam_im) / den
    fi = (ni * lam_re - nr * lam_im) / den
    b_re = b_re.astype(F32)
    b_im = b_im.astype(F32)
    bbr = fr[..., None] * b_re - fi[..., None] * b_im
    bbi = fr[..., None] * b_im + fi[..., None] * b_re

    eye = jnp.eye(g_n, dtype=F32)
    bb = jnp.stack([bbr, bbi], axis=0)
    dense_b = jnp.einsum('cgph,kg->khcgp', bb, eye)
    dense_b = dense_b.reshape(D_SSM, 2, N_STATE_TILES, 2, p_n).transpose(0, 2, 1, 3, 4)
    dense_b = dense_b.reshape(D_SSM, STATE_LANES)
    cc = jnp.stack([c_re.astype(F32), -c_im.astype(F32)], axis=0)
    dense_c = jnp.einsum('cghp,kg->cgpkh', cc, eye)
    dense_c = dense_c.reshape(2, N_STATE_TILES, 2, p_n, D_SSM).transpose(1, 0, 2, 3, 4)
    dense_c = dense_c.reshape(STATE_LANES, D_SSM)

    half = STATE_LANES // 2
    bexp = jnp.stack([dense_b[kt * MXU_TILE:(kt + 1) * MXU_TILE, kt * half:(kt + 1) * half]
                      for kt in range(2)])
    cexp = jnp.stack([dense_c[kt * half:(kt + 1) * half, kt * MXU_TILE:(kt + 1) * MXU_TILE]
                      for kt in range(2)])

    rows = jnp.arange(SUBLANES)[:, None]
    tabs = []
    for shift in (1, 2, 4):
        v = to_lanes(*power(float(shift)))
        tabs.append(jnp.where(rows >= shift, v[None, :], 0.0))
    tabs.append(jnp.stack([to_lanes(*power(float(i + 1))) for i in range(SUBLANES)]))
    coef = jnp.stack(tabs).astype(F32)
    return bexp.astype(BF16), cexp.astype(BF16), coef


def _attn_body(lam_ref, q_ref, k_ref, vt_ref, g_ref, o_ref, acc_ref, *, tq, tk, lambda_init):
    qi = pl.program_id(2)
    qt = q_ref[0].astype(F32).T.astype(BF16)
    row = lax.broadcasted_iota(I32, qt.shape, 0)
    zero = jnp.zeros_like(qt)
    qt_maps = (jnp.where(row < HEAD_DIM_QK, qt, zero), jnp.where(row >= HEAD_DIM_QK, qt, zero))
    acc_ref[...] = jnp.zeros(acc_ref.shape, F32)

    def step(ki, stats, mask):
        start = pl.multiple_of(ki * tk, tk)
        kb = k_ref[0, pl.ds(start, tk), :]
        vtb = vt_ref[:, pl.ds(start, tk)]
        new_stats = []
        for m in range(2):
            m_old, l_old = stats[m]
            s = jnp.dot(kb, qt_maps[m], preferred_element_type=F32)
            if mask is not None:
                s = jnp.where(mask, s, -jnp.inf)
            m_new = jnp.maximum(m_old, jnp.max(s, axis=0, keepdims=True))
            p = jnp.exp2(s - m_new)
            alpha = jnp.exp2(m_old - m_new)
            l_new = alpha * l_old + jnp.sum(p, axis=0, keepdims=True)
            acc_ref[m] = alpha * acc_ref[m] + jnp.dot(vtb, p.astype(BF16),
                                                      preferred_element_type=F32)
            new_stats.append((m_new, l_new))
        return tuple(new_stats)

    init = tuple((jnp.full((1, tq), -jnp.inf, F32), jnp.zeros((1, tq), F32)) for _ in range(2))
    n_full = qi * (tq // tk)
    stats = lax.fori_loop(0, n_full, lambda ki, st: step(ki, st, None), init)
    qry_chunk = lax.broadcasted_iota(I32, (tk, tq), 1) // CHUNK
    for d in range(tq // tk):
        key_chunk = (lax.broadcasted_iota(I32, (tk, tq), 0) + d * tk) // CHUNK
        stats = step(n_full + d, stats, key_chunk <= qry_chunk)
    (_, l0), (_, l1) = stats

    ot = acc_ref[0] / l0 - lam_ref[0] * (acc_ref[1] / l1)
    ms = jnp.mean(ot * ot, axis=0, keepdims=True)
    gain = jnp.concatenate([g_ref[...]] * (tq // LANES), axis=1)
    ot = ot * lax.rsqrt(ms + EPS) * gain * (1.0 - lambda_init)
    o_ref[0] = ot.T.astype(BF16)


def _attention(lam, q, k, vt, subln_g, bsz, seq, tq, tk, lambda_init):
    q3 = q.reshape(bsz, seq, D_ATTN)
    k3 = k.reshape(bsz, seq, D_ATTN)
    gain = jnp.broadcast_to(subln_g.astype(F32)[:, None], (HEAD_DIM_V, LANES))
    qmap = lambda b, h, i: (b, i, h)
    out = pl.pallas_call(
        functools.partial(_attn_body, tq=tq, tk=tk, lambda_init=lambda_init),
        grid=(bsz, N_HEADS, seq // tq),
        in_specs=[
            pl.BlockSpec(memory_space=pltpu.SMEM),
            pl.BlockSpec((1, tq, HEAD_DIM_V), qmap),
            pl.BlockSpec((1, seq, HEAD_DIM_V), lambda b, h, i: (b, 0, h)),
            pl.BlockSpec((HEAD_DIM_V, seq), lambda b, h, i: (h, b)),
            pl.BlockSpec((HEAD_DIM_V, LANES), lambda b, h, i: (0, 0)),
        ],
        out_specs=pl.BlockSpec((1, tq, HEAD_DIM_V), qmap),
        out_shape=jax.ShapeDtypeStruct((bsz, seq, D_ATTN), BF16),
        scratch_shapes=[pltpu.VMEM((2, HEAD_DIM_V, tq), F32)],
        compiler_params=_cparams(("arbitrary", "arbitrary", "arbitrary")),
        name="diff_attention",
    )(lam, q3, k3, vt, gain)
    return out.reshape(bsz * seq, D_ATTN)


def _out_router_body(h_ref, ys_ref, ya_ref, wo_ref, g_ref, wr_hi_ref, wr_lo_ref, br_ref,
                     h1_ref, xp_ref, idx_ref, gate_ref, rank_ref, cnt_ref, base_ref, *, tm):
    @pl.when(pl.program_id(0) == 0)
    def _():
        base_ref[...] = jnp.zeros_like(base_ref)

    h1 = (h_ref[...]
          + jnp.dot(ys_ref[...], wo_ref[:D_SSM, :], preferred_element_type=F32)
          + jnp.dot(ya_ref[...], wo_ref[D_SSM:, :], preferred_element_type=F32))
    h1_ref[...] = h1
    ms = jnp.mean(h1 * h1, axis=-1, keepdims=True)
    xn = h1 * lax.rsqrt(ms + EPS) * g_ref[...]
    _store_token_tiles(xp_ref, xn)

    x_hi = xn.astype(BF16)
    x_lo = (xn - x_hi.astype(F32)).astype(BF16)
    nt = (((1,), (1,)), ((), ()))
    logits = (lax.dot_general(wr_hi_ref[...], x_hi, nt, preferred_element_type=F32)
              + lax.dot_general(wr_hi_ref[...], x_lo, nt, preferred_element_type=F32)
              + lax.dot_general(wr_lo_ref[...], x_hi, nt, preferred_element_type=F32)
              + br_ref[...])

    e_iota = lax.broadcasted_iota(I32, logits.shape, 0)
    vals = logits
    top_v, top_i = [], []
    for _ in range(TOP_K):
        mx = jnp.max(vals, axis=0, keepdims=True)
        idx = jnp.min(jnp.where(vals == mx, e_iota, N_EXPERTS), axis=0, keepdims=True)
        top_v.append(mx)
        top_i.append(idx)
        vals = jnp.where(e_iota == idx, -jnp.inf, vals)
    ex = [jnp.exp(v - top_v[0]) for v in top_v]
    den = ex[0] + ex[1] + ex[2] + ex[3]
    gate_ref[...] = jnp.concatenate([e / den for e in ex], axis=0)
    idx_ref[...] = jnp.concatenate(top_i, axis=0)

    onehots = [e_iota == i for i in top_i]
    cnt = jnp.zeros(logits.shape, F32)
    for oh in onehots:
        cnt = cnt + oh.astype(F32)
    tri = (lax.broadcasted_iota(I32, (tm, tm), 0)
           <= lax.broadcasted_iota(I32, (tm, tm), 1)).astype(BF16)
    incl = jnp.dot(cnt.astype(BF16), tri, preferred_element_type=F32)
    base = base_ref[:, 0:1]
    excl = incl - cnt + base
    ranks = [jnp.sum(jnp.where(oh, excl, 0.0), axis=0, keepdims=True) for oh in onehots]
    rank_ref[...] = jnp.concatenate(ranks, axis=0).astype(I32)
    new_base = base + incl[:, tm - 1:tm]
    base_ref[...] = jnp.broadcast_to(new_base, base_ref.shape)
    cnt_ref[...] = jnp.broadcast_to(new_base, cnt_ref.shape)


def _out_router(h2, ys, ya, wo_bf, g, wr_hi, wr_lo, br, tm):
    n_tok = h2.shape[0]
    tok = lambda i: (i, 0)
    col = lambda i: (0, i)
    const = lambda i: (0, 0)
    return pl.pallas_call(
        functools.partial(_out_router_body, tm=tm),
        grid=(n_tok // tm,),
        in_specs=[
            pl.BlockSpec((tm, D_MODEL), tok),
            pl.BlockSpec((tm, D_SSM), tok),
            pl.BlockSpec((tm, D_ATTN), tok),
            pl.BlockSpec((D_MODEL, D_MODEL), const),
            pl.BlockSpec((1, D_MODEL), const),
            pl.BlockSpec((N_EXPERTS, D_MODEL), const),
            pl.BlockSpec((N_EXPERTS, D_MODEL), const),
            pl.BlockSpec((N_EXPERTS, 1), const),
        ],
        out_specs=[
            pl.BlockSpec((tm, D_MODEL), tok),
            pl.BlockSpec((tm * ROW_TILE, LANES), tok),
            pl.BlockSpec((TOP_K, tm), col),
            pl.BlockSpec((TOP_K, tm), col),
            pl.BlockSpec((TOP_K, tm), col),
            pl.BlockSpec((N_EXPERTS, LANES), const),
        ],
        out_shape=[
            jax.ShapeDtypeStruct((n_tok, D_MODEL), F32),
            jax.ShapeDtypeStruct((n_tok * ROW_TILE, LANES), F32),
            jax.ShapeDtypeStruct((TOP_K, n_tok), I32),
            jax.ShapeDtypeStruct((TOP_K, n_tok), F32),
            jax.ShapeDtypeStruct((TOP_K, n_tok), I32),
            jax.ShapeDtypeStruct((N_EXPERTS, LANES), F32),
        ],
        scratch_shapes=[pltpu.VMEM((N_EXPERTS, LANES), F32)],
        compiler_params=_cparams(("arbitrary",)),
        name="out_proj_router",
    )(h2, ys, ya, wo_bf, g, wr_hi, wr_lo, br)


def _row_tile(ref, r):
    return ref.at[pl.ds(pl.multiple_of(r * ROW_TILE, ROW_TILE), ROW_TILE), :]


def _dispatch_body(zs_ref, zf_ref, dest_ref, x_ref, xs_hbm, zero_ref, sem, zsem, *, tm, rows):
    @pl.when(pl.program_id(0) == 0)
    def _():
        zero_ref[...] = jnp.zeros_like(zero_ref)

        def fill(e):
            start = pl.multiple_of(zs_ref[e] * ROW_TILE, ROW_TILE)
            return pltpu.make_async_copy(zero_ref, xs_hbm.at[pl.ds(start, rows * ROW_TILE), :], zsem)

        for e in range(2 * N_EXPERTS):
            @pl.when(zf_ref[e] > 0)
            def _():
                fill(e).start()
        for e in range(2 * N_EXPERTS):
            @pl.when(zf_ref[e] > 0)
            def _():
                fill(e).wait()

    def issue(t, c):
        for k in range(TOP_K):
            d = dest_ref[0, 0, k * tm + t]
            pltpu.make_async_copy(_row_tile(x_ref, t), _row_tile(xs_hbm, d), sem).start(priority=k % 2)
        return c

    lax.fori_loop(0, tm, issue, 0, unroll=8)
    for k in range(TOP_K):
        pltpu.make_async_copy(x_ref, xs_hbm.at[pl.ds(0, tm * ROW_TILE), :], sem).wait()


def _dispatch(zstart, zflag, dest3, xp, n_rows, tm, rows):
    n_tok = xp.shape[0] // ROW_TILE
    return pl.pallas_call(
        functools.partial(_dispatch_body, tm=tm, rows=rows),
        grid_spec=pltpu.PrefetchScalarGridSpec(
            num_scalar_prefetch=2,
            grid=(n_tok // tm,),
            in_specs=[
                pl.BlockSpec((1, 1, TOP_K * tm), lambda i, zs, zf: (i, 0, 0),
                             memory_space=pltpu.SMEM),
                pl.BlockSpec((tm * ROW_TILE, LANES), lambda i, zs, zf: (i, 0)),
            ],
            out_specs=pl.BlockSpec(memory_space=pl.ANY),
            scratch_shapes=[
                pltpu.VMEM((rows * ROW_TILE, LANES), F32),
                pltpu.SemaphoreType.DMA,
                pltpu.SemaphoreType.DMA,
            ],
        ),
        out_shape=jax.ShapeDtypeStruct((n_rows * ROW_TILE, LANES), F32),
        compiler_params=_cparams(("arbitrary",)),
        name="moe_dispatch",
    )(zstart, zflag, dest3, xp)


def _expert_body(be_ref, nu_ref, xs_ref, wu_ref, bu_ref, wd_ref, bd_ref, y_ref,
                 wu_bf, wd_bf, *, rows):
    i = pl.program_id(0)
    prev = be_ref[jnp.maximum(i - 1, 0)]

    @pl.when((i == 0) | (be_ref[i] != prev))
    def _():
        wu_bf[...] = wu_ref[0, 0].astype(BF16)
        wd_bf[...] = wd_ref[0, 0].astype(BF16)

    @pl.when(i < nu_ref[0])
    def _():
        x = _load_token_tiles(xs_ref, rows).astype(BF16)
        h = jnp.dot(x, wu_bf[...], preferred_element_type=F32) + bu_ref[0, 0]
        glu = jnp.minimum(h[:, :D_FF], SWIGLU_LIMIT)
        lin = jnp.clip(h[:, D_FF:], -SWIGLU_LIMIT, SWIGLU_LIMIT)
        act = glu * jax.nn.sigmoid(SWIGLU_ALPHA * glu) * (lin + 1.0)
        y = jnp.dot(act.astype(BF16), wd_bf[...], preferred_element_type=F32) + bd_ref[0, 0]
        _store_token_tiles(y_ref, y)

    @pl.when(i >= nu_ref[0])
    def _():
        y_ref[...] = jnp.zeros_like(y_ref)


def _experts(block_expert, n_used, xs, w_up, b_up, w_down, b_down, layer, rows):
    n_blocks = xs.shape[0] // (rows * ROW_TILE)
    live = lambda i, be, nu: (jnp.minimum(i, nu[0] - 1), 0)
    wmap = lambda i, be, nu: (layer, be[i], 0, 0)
    return pl.pallas_call(
        functools.partial(_expert_body, rows=rows),
        grid_spec=pltpu.PrefetchScalarGridSpec(
            num_scalar_prefetch=2,
            grid=(n_blocks,),
            in_specs=[
                pl.BlockSpec((rows * ROW_TILE, LANES), live),
                pl.BlockSpec((1, 1, D_MODEL, 2 * D_FF), wmap),
                pl.BlockSpec((1, 1, 1, 2 * D_FF), wmap),
                pl.BlockSpec((1, 1, D_FF, D_MODEL), wmap),
                pl.BlockSpec((1, 1, 1, D_MODEL), wmap),
            ],
            out_specs=pl.BlockSpec((rows * ROW_TILE, LANES), lambda i, be, nu: (i, 0)),
            scratch_shapes=[
                pltpu.VMEM((D_MODEL, 2 * D_FF), BF16),
                pltpu.VMEM((D_FF, D_MODEL), BF16),
            ],
        ),
        out_shape=jax.ShapeDtypeStruct(xs.shape, F32),
        compiler_params=_cparams(("arbitrary",)),
        name="moe_experts",
    )(block_expert, n_used, xs, w_up, b_up, w_down, b_down)


def _combine_body(dest_ref, next_ref, h1_ref, gate_ref, y_hbm, o_ref, buf_ref, sems, *, tm):
    i = pl.program_id(0)
    slot = i % 2
    other = 1 - slot
    span = TOP_K * tm * ROW_TILE

    def slot_rows(s, first, n):
        return buf_ref.at[pl.ds(pl.multiple_of(s * span + first * ROW_TILE, ROW_TILE), n), :]

    def start_gathers(idx_ref, t, s):
        for k in range(TOP_K):
            d = idx_ref[0, 0, k * tm + t]
            pltpu.make_async_copy(_row_tile(y_hbm, d), slot_rows(s, k * tm + t, ROW_TILE),
                                  sems.at[s]).start(priority=k % 2)

    def wait_slot(s):
        pltpu.make_async_copy(y_hbm.at[pl.ds(0, span), :], slot_rows(s, 0, span), sems.at[s]).wait()

    @pl.when(i == 0)
    def _():
        def first(t, c):
            start_gathers(dest_ref, t, 0)
            return c
        lax.fori_loop(0, tm, first, 0, unroll=8)

    wait_slot(slot)

    def body(g, c):
        t0 = pl.multiple_of(g * SUBLANES, SUBLANES)
        gates = gate_ref[pl.ds(t0, SUBLANES), :]
        gate_cols = [gates[:, k:k + 1] for k in range(TOP_K)]
        for cc in range(ROW_TILE):
            lanes = slice(cc * LANES, (cc + 1) * LANES)
            acc = h1_ref[pl.ds(t0, SUBLANES), lanes]
            for k in range(TOP_K):
                first_row = slot * span + (k * tm + t0) * ROW_TILE + cc
                acc = acc + gate_cols[k] * buf_ref[pl.ds(first_row, SUBLANES, stride=ROW_TILE), :]
            o_ref[pl.ds(t0, SUBLANES), lanes] = acc
        for j in range(SUBLANES):
            start_gathers(next_ref, t0 + j, other)
        return c

    lax.fori_loop(0, tm // SUBLANES, body, 0)

    @pl.when(i == pl.num_programs(0) - 1)
    def _():
        wait_slot(other)


def _combine(dest3, h1, gates_t, y_buf, tm):
    n_tok = h1.shape[0]
    n_steps = n_tok // tm
    tok = lambda i: (i, 0)
    return pl.pallas_call(
        functools.partial(_combine_body, tm=tm),
        grid=(n_steps,),
        in_specs=[
            pl.BlockSpec((1, 1, TOP_K * tm), lambda i: (i, 0, 0), memory_space=pltpu.SMEM),
            pl.BlockSpec((1, 1, TOP_K * tm), lambda i: (jnp.minimum(i + 1, n_steps - 1), 0, 0),
                         memory_space=pltpu.SMEM),
            pl.BlockSpec((tm, D_MODEL), tok),
            pl.BlockSpec((tm, TOP_K), tok),
            pl.BlockSpec(memory_space=pl.ANY),
        ],
        out_specs=pl.BlockSpec((tm, D_MODEL), tok),
        out_shape=jax.ShapeDtypeStruct((n_tok, D_MODEL), F32),
        scratch_shapes=[
            pltpu.VMEM((2 * TOP_K * tm * ROW_TILE, LANES), F32),
            pltpu.SemaphoreType.DMA((2,)),
        ],
        compiler_params=_cparams(("arbitrary",)),
        name="moe_combine",
    )(dest3, dest3, h1, gates_t, y_buf)


def _tiles(seq, n_tok):
    tm = min(512, seq)
    tq = min(1024, seq)
    tk = min(1024, seq)
    tr = min(512, seq)
    rows = min(512, n_tok // 8)
    return tm, tq, tk, tr, rows


def _rope_tables(seq):
    inv = 1.0 / (ROPE_THETA ** (jnp.arange(0, HEAD_DIM_QK, 2, dtype=F32) / HEAD_DIM_QK))
    ang = jnp.arange(seq, dtype=F32)[:, None] * inv[None, :]
    cos, sin = jnp.cos(ang), jnp.sin(ang)
    reps = LANES // HEAD_DIM_QK
    cos_t = jnp.concatenate([cos, cos] * reps, axis=1)
    sin_t = jnp.concatenate([-sin, sin] * reps, axis=1)
    return cos_t, sin_t


def kernel(x, g_mix, w_in, ssm_lam_re, ssm_lam_im, ssm_log_step, ssm_b_re, ssm_b_im, ssm_c_re, ssm_c_im, ssm_d, ssm_w_glu, ssm_norm_g, q_norm_g, k_norm_g, lam_q1, lam_k1, lam_q2, lam_k2, subln_g, w_out, g_ffn, w_router, b_router, w_up, b_up, w_down, b_down):
    bsz, seq, _ = x.shape
    n_tok = bsz * seq
    depth = w_in.shape[0]
    tm, tq, tk, tr, rows = _tiles(seq, n_tok)
    n_blocks = (n_tok * TOP_K) // rows + N_EXPERTS
    n_rows = n_blocks * rows

    cos_t, sin_t = _rope_tables(seq)
    seg = jnp.kron(jnp.eye(D_ATTN // HEAD_DIM_QK, dtype=F32),
                   jnp.full((HEAD_DIM_QK, HEAD_DIM_QK), 1.0 / HEAD_DIM_QK, F32)).astype(BF16)

    h = x.reshape(n_tok, D_MODEL).astype(F32)
    for l in range(depth):
        lambda_init = 0.8 - 0.6 * math.exp(-0.3 * l)
        tile_g = lambda g: jnp.tile(g.astype(F32), D_ATTN // HEAD_DIM_QK)[None, :]
        w_uqk = w_in[l][:, :D_IN - D_ATTN].astype(BF16)
        w_vt = w_in[l][:, D_IN - D_ATTN:].T.astype(BF16)
        u, q, k, vt = _in_proj(h, g_mix[l][None, :].astype(F32), w_uqk, w_vt, cos_t, sin_t,
                               tile_g(q_norm_g[l]), tile_g(k_norm_g[l]), seg, seq, tm)

        bexp, cexp, coef = _ssm_params(ssm_lam_re[l], ssm_lam_im[l], ssm_log_step[l],
                                       ssm_b_re[l], ssm_b_im[l], ssm_c_re[l], ssm_c_im[l])
        y_ssm = _ssm(u, bexp, cexp, coef, ssm_d[l][None, :].astype(F32),
                     ssm_w_glu[l].astype(BF16), ssm_norm_g[l][None, :].astype(F32), bsz, seq, tm)

        lam = (jnp.exp(jnp.sum(lam_q1[l].astype(F32) * lam_k1[l].astype(F32)))
               - jnp.exp(jnp.sum(lam_q2[l].astype(F32) * lam_k2[l].astype(F32)))
               + lambda_init).reshape(1).astype(F32)
        y_att = _attention(lam, q, k, vt, subln_g[l], bsz, seq, tq, tk, lambda_init)

        wr_t = w_router[l].astype(F32).T
        wr_hi = wr_t.astype(BF16)
        wr_lo = (wr_t - wr_hi.astype(F32)).astype(BF16)
        h1, xp, top_i, gates, rank, cnt = _out_router(
            h, y_ssm, y_att, w_out[l].astype(BF16), g_ffn[l][None, :].astype(F32),
            wr_hi, wr_lo, b_router[l].astype(F32)[:, None], tr)

        counts = cnt[:, 0].astype(I32)
        padded = (counts + rows - 1) // rows * rows
        pad_end = jnp.cumsum(padded)
        pad_start = pad_end - padded
        e_ids = jnp.arange(N_EXPERTS, dtype=I32)
        start_of = jnp.sum(jnp.where(top_i[None] == e_ids[:, None, None],
                                     pad_start[:, None, None], 0), axis=0)
        dest = start_of + rank
        dest3 = dest.reshape(TOP_K, n_tok // tr, tr).transpose(1, 0, 2).reshape(n_tok // tr, 1, TOP_K * tr)
        block_row = jnp.arange(n_blocks, dtype=I32) * rows
        block_expert = jnp.minimum(
            jnp.sum((pad_end[None, :] <= block_row[:, None]).astype(I32), axis=1),
            N_EXPERTS - 1).astype(I32)
        n_used = (pad_end[-1] // rows).reshape(1).astype(I32)
        spare = n_blocks - 1 - jnp.arange(N_EXPERTS, dtype=I32)
        zstart = jnp.concatenate([jnp.maximum(pad_end - rows, 0), spare * rows]).astype(I32)
        zflag = jnp.concatenate([padded > 0, spare >= n_used[0]]).astype(I32)

        xs = _dispatch(zstart, zflag, dest3, xp, n_rows, tr, rows)
        y_buf = _experts(block_expert, n_used, xs, w_up, b_up[:, :, None, :],
                         w_down, b_down[:, :, None, :], l, rows)
        h = _combine(dest3, h1, gates.T, y_buf, tr)
    return h.reshape(bsz, seq, D_MODEL).astype(x.dtype)
```

```python
import functools
import math

import jax
import jax.numpy as jnp
from jax import lax
from jax.experimental import pallas as pl
from jax.experimental.pallas import tpu as pltpu

F32 = jnp.float32
BF16 = jnp.bfloat16
I32 = jnp.int32

D_MODEL = 1024
CHUNK = 64
D_SSM = 512
SSM_GROUP = 16
N_GROUPS = 32
SSM_STATE = 64
D_ATTN = 512
N_HEADS = 4
HEAD_DIM_V = 128
HEAD_DIM_QK = 64
D_IN = D_SSM + 3 * D_ATTN
ROPE_THETA = 10000.0
N_EXPERTS = 32
TOP_K = 4
D_FF = 1024
SWIGLU_LIMIT = 7.0
SWIGLU_ALPHA = 1.702
EPS = 1e-6

LANES = 128
SUBLANES = 8
MXU_TILE = 256
STATE_LANES = 2 * N_GROUPS * SSM_STATE
N_STATE_TILES = STATE_LANES // MXU_TILE
ROW_TILE = D_MODEL // LANES
VMEM_LIMIT = 56 * 1024 * 1024


def _cparams(sem, vmem=VMEM_LIMIT):
    return pltpu.CompilerParams(dimension_semantics=sem, vmem_limit_bytes=vmem)


def _store_token_tiles(ref, val, base=0):
    n = val.shape[0]
    for c in range(ROW_TILE):
        ref[pl.ds(base + c, n, stride=ROW_TILE), :] = val[:, c * LANES:(c + 1) * LANES]


def _load_token_tiles(ref, n, base=0):
    return jnp.concatenate(
        [ref[pl.ds(base + c, n, stride=ROW_TILE), :] for c in range(ROW_TILE)], axis=1)


def _in_proj_body(x_ref, g_ref, w_ref, wvt_ref, cos_ref, sin_ref, qg_ref, kg_ref, seg_ref,
                  u_ref, q_ref, k_ref, vt_ref):
    x = x_ref[...]
    ms = jnp.mean(x * x, axis=-1, keepdims=True)
    xn = (x * lax.rsqrt(ms + EPS) * g_ref[...]).astype(BF16)
    proj = jnp.dot(xn, w_ref[...], preferred_element_type=F32)
    u_ref[...] = proj[:, :D_SSM].astype(BF16)
    vt_ref[...] = lax.dot_general(wvt_ref[...], xn, (((1,), (1,)), ((), ())),
                                  preferred_element_type=F32).astype(BF16)

    reps = D_ATTN // LANES
    cos = jnp.concatenate([cos_ref[...]] * reps, axis=1)
    sin = jnp.concatenate([sin_ref[...]] * reps, axis=1)
    lane = lax.broadcasted_iota(I32, cos.shape, 1)
    first_half = (lane & (HEAD_DIM_QK // 2)) == 0

    def prep(z, gain):
        ms = jnp.dot((z * z).astype(BF16), seg_ref[...], preferred_element_type=F32)
        zn = z * lax.rsqrt(ms + EPS) * gain
        half = HEAD_DIM_QK // 2
        swapped = jnp.where(first_half, pltpu.roll(zn, D_ATTN - half, 1), pltpu.roll(zn, half, 1))
        return zn * cos + swapped * sin

    scale = math.log2(math.e) / math.sqrt(HEAD_DIM_QK)
    q_ref[...] = (prep(proj[:, D_SSM:D_SSM + D_ATTN], qg_ref[...]) * scale).astype(BF16)
    k_ref[...] = prep(proj[:, D_SSM + D_ATTN:D_SSM + 2 * D_ATTN], kg_ref[...]).astype(BF16)


def _in_proj(x2, g, w_bf, wvt_bf, cos_t, sin_t, qg, kg, seg, seq, tm):
    n_tok = x2.shape[0]
    n_pos = seq // tm
    tok = lambda i: (i, 0)
    const = lambda i: (0, 0)
    pos = lambda i: (i % n_pos, 0)
    out = jax.ShapeDtypeStruct((n_tok, D_SSM), BF16)
    out_t = jax.ShapeDtypeStruct((D_ATTN, n_tok), BF16)
    return pl.pallas_call(
        _in_proj_body,
        grid=(n_tok // tm,),
        in_specs=[
            pl.BlockSpec((tm, D_MODEL), tok),
            pl.BlockSpec((1, D_MODEL), const),
            pl.BlockSpec((D_MODEL, D_IN - D_ATTN), const),
            pl.BlockSpec((D_ATTN, D_MODEL), const),
            pl.BlockSpec((tm, LANES), pos),
            pl.BlockSpec((tm, LANES), pos),
            pl.BlockSpec((1, D_ATTN), const),
            pl.BlockSpec((1, D_ATTN), const),
            pl.BlockSpec((D_ATTN, D_ATTN), const),
        ],
        out_specs=[pl.BlockSpec((tm, D_SSM), tok)] * 3 + [pl.BlockSpec((D_ATTN, tm), lambda i: (0, i))],
        out_shape=[out] * 3 + [out_t],
        compiler_params=_cparams(("arbitrary",)),
        name="in_proj",
    )(x2, g, w_bf, wvt_bf, cos_t, sin_t, qg, kg, seg)


FOLD_TILES = STATE_LANES // LANES
FOLD_PITCH = 40


def _ssm_body(u_ref, bexp_ref, cexp_ref, lam_ref, d_ref, wglu_ref, gn_ref, o_ref,
              fold_ref, state_ref, *, tm, nb):
    @pl.when(pl.program_id(0) == 0)
    def _():
        state_ref[...] = jnp.zeros_like(state_ref)

    half = FOLD_TILES // 2
    slab = half // 2
    chunk = 4 * LANES

    def fold_rows(b, tile):
        return pl.ds(b * tm * FOLD_PITCH + tile, tm, stride=FOLD_PITCH)

    def slab_tiles(kt):
        return ([kt * slab + j for j in range(slab)]
                + [half + kt * slab + j for j in range(slab)])

    for b in range(nb):
        for kt in range(2):
            uk = u_ref[b, :, kt * MXU_TILE:(kt + 1) * MXU_TILE]
            tiles = slab_tiles(kt)
            for q in range(len(tiles) * LANES // chunk):
                res = jnp.dot(uk, bexp_ref[kt, :, q * chunk:(q + 1) * chunk],
                              preferred_element_type=F32)
                for i in range(chunk // LANES):
                    fold_ref[fold_rows(b, tiles[q * (chunk // LANES) + i]), :] = (
                        res[:, i * LANES:(i + 1) * LANES])

    lam_r = lam_ref[0:half, :]
    lam_i = lam_ref[half:FOLD_TILES, :]

    def body(t, carry):
        new = []
        for b in range(nb):
            xr, xi = carry[b]
            base = pl.multiple_of((b * tm + t) * FOLD_PITCH, SUBLANES)
            nr = lam_r * xr - lam_i * xi + fold_ref[pl.ds(base, half), :]
            ni = lam_r * xi + lam_i * xr + fold_ref[pl.ds(base + half, half), :]
            fold_ref[pl.ds(base, half), :] = nr
            fold_ref[pl.ds(base + half, half), :] = ni
            new.append((nr, ni))
        return tuple(new)

    init = tuple((state_ref[b, 0:half, :], state_ref[b, half:FOLD_TILES, :]) for b in range(nb))
    final = lax.fori_loop(0, tm, body, init, unroll=2)
    for b in range(nb):
        state_ref[b, 0:half, :] = final[b][0]
        state_ref[b, half:FOLD_TILES, :] = final[b][1]

    for b in range(nb):
        parts = []
        for n in range(2):
            xs = jnp.concatenate([fold_ref[fold_rows(b, tile), :] for tile in slab_tiles(n)],
                                 axis=1).astype(BF16)
            parts.append(jnp.dot(xs, cexp_ref[n], preferred_element_type=F32))
        y = jnp.concatenate(parts, axis=1) + d_ref[...] * u_ref[b].astype(F32)
        y = jax.nn.gelu(y)
        y = y * jax.nn.sigmoid(jnp.dot(y.astype(BF16), wglu_ref[...], preferred_element_type=F32))
        ms = jnp.mean(y * y, axis=-1, keepdims=True)
        o_ref[b] = (y * lax.rsqrt(ms + EPS) * gn_ref[...]).astype(BF16)


def _ssm(u, bexp, cexp, lam_tab, d, wglu_bf, gn, bsz, seq, tm):
    u3 = u.reshape(bsz, seq, D_SSM)
    tok = lambda s: (0, s, 0)
    c2 = lambda s: (0, 0)
    c3 = lambda s: (0, 0, 0)
    out = pl.pallas_call(
        functools.partial(_ssm_body, tm=tm, nb=bsz),
        grid=(seq // tm,),
        in_specs=[
            pl.BlockSpec((bsz, tm, D_SSM), tok),
            pl.BlockSpec((2, MXU_TILE, STATE_LANES // 2), c3),
            pl.BlockSpec((2, STATE_LANES // 2, MXU_TILE), c3),
            pl.BlockSpec((FOLD_TILES, LANES), c2),
            pl.BlockSpec((1, D_SSM), c2),
            pl.BlockSpec((D_SSM, D_SSM), c2),
            pl.BlockSpec((1, D_SSM), c2),
        ],
        out_specs=pl.BlockSpec((bsz, tm, D_SSM), tok),
        out_shape=jax.ShapeDtypeStruct((bsz, seq, D_SSM), BF16),
        scratch_shapes=[
            pltpu.VMEM((bsz * tm * FOLD_PITCH, LANES), F32),
            pltpu.VMEM((bsz, FOLD_TILES, LANES), F32),
        ],
        compiler_params=_cparams(("arbitrary",)),
        name="ssm_mixer",
    )(u3, bexp, cexp, lam_tab, d, wglu_bf, gn)
    return out.reshape(bsz * seq, D_SSM)


def _ssm_params(lam_re, lam_im, log_step, b_re, b_im, c_re, c_im):
    g_n = N_GROUPS
    step = jnp.exp(log_step.astype(F32))[:, None]
    lam_re = lam_re.astype(F32)
    lam_im = lam_im.astype(F32)
    mag = jnp.exp(lam_re * step)
    lbr = mag * jnp.cos(lam_im * step)
    lbi = mag * jnp.sin(lam_im * step)
    nr, ni = lbr - 1.0, lbi
    den = lam_re * lam_re + lam_im * lam_im
    fr = (nr * lam_re + ni * lam_im) / den
    fi = (ni * lam_re - nr * lam_im) / den
    b_re = b_re.astype(F32)
    b_im = b_im.astype(F32)
    bbr = fr[..., None] * b_re - fi[..., None] * b_im
    bbi = fr[..., None] * b_im + fi[..., None] * b_re

    eye = jnp.eye(g_n, dtype=F32)
    bb = jnp.stack([bbr, bbi], axis=0)
    dense_b = jnp.einsum('cgph,kg->khcgp', bb, eye).reshape(D_SSM, STATE_LANES)
    cc = jnp.stack([c_re.astype(F32), -c_im.astype(F32)], axis=0)
    dense_c = jnp.einsum('cghp,kg->cgpkh', cc, eye).reshape(STATE_LANES, D_SSM)

    part = STATE_LANES // 2
    span = part // 2
    bexp = jnp.stack([
        jnp.concatenate([dense_b[kt * MXU_TILE:(kt + 1) * MXU_TILE,
                                 c * part + kt * span:c * part + (kt + 1) * span]
                         for c in range(2)], axis=1) for kt in range(2)])
    cexp = jnp.stack([
        jnp.concatenate([dense_c[c * part + kt * span:c * part + (kt + 1) * span,
                                 kt * MXU_TILE:(kt + 1) * MXU_TILE]
                         for c in range(2)], axis=0) for kt in range(2)])
    lam_tab = jnp.concatenate([lbr.reshape(FOLD_TILES // 2, LANES),
                               lbi.reshape(FOLD_TILES // 2, LANES)], axis=0)
    return bexp.astype(BF16), cexp.astype(BF16), lam_tab.astype(F32)


def _attn_body(lam_ref, q_ref, k_ref, vt_ref, g_ref, o_ref, acc_ref, *, tq, tk, lambda_init):
    qi = pl.program_id(2)
    qt = q_ref[0].astype(F32).T.astype(BF16)
    row = lax.broadcasted_iota(I32, qt.shape, 0)
    zero = jnp.zeros_like(qt)
    qt_maps = (jnp.where(row < HEAD_DIM_QK, qt, zero), jnp.where(row >= HEAD_DIM_QK, qt, zero))
    acc_ref[...] = jnp.zeros(acc_ref.shape, F32)

    def step(ki, stats, mask):
        start = pl.multiple_of(ki * tk, tk)
        kb = k_ref[0, pl.ds(start, tk), :]
        vtb = vt_ref[:, pl.ds(start, tk)]
        new_stats = []
        for m in range(2):
            m_old, l_old = stats[m]
            s = jnp.dot(kb, qt_maps[m], preferred_element_type=F32)
            if mask is not None:
                s = jnp.where(mask, s, -jnp.inf)
            m_new = jnp.maximum(m_old, jnp.max(s, axis=0, keepdims=True))
            p = jnp.exp2(s - m_new)
            alpha = jnp.exp2(m_old - m_new)
            l_new = alpha * l_old + jnp.sum(p, axis=0, keepdims=True)
            acc_ref[m] = alpha * acc_ref[m] + jnp.dot(vtb, p.astype(BF16),
                                                      preferred_element_type=F32)
            new_stats.append((m_new, l_new))
        return tuple(new_stats)

    init = tuple((jnp.full((1, tq), -jnp.inf, F32), jnp.zeros((1, tq), F32)) for _ in range(2))
    n_full = qi * (tq // tk)
    stats = lax.fori_loop(0, n_full, lambda ki, st: step(ki, st, None), init)
    qry_chunk = lax.broadcasted_iota(I32, (tk, tq), 1) // CHUNK
    for d in range(tq // tk):
        key_chunk = (lax.broadcasted_iota(I32, (tk, tq), 0) + d * tk) // CHUNK
        stats = step(n_full + d, stats, key_chunk <= qry_chunk)
    (_, l0), (_, l1) = stats

    ot = acc_ref[0] / l0 - lam_ref[0] * (acc_ref[1] / l1)
    ms = jnp.mean(ot * ot, axis=0, keepdims=True)
    gain = jnp.concatenate([g_ref[...]] * (tq // LANES), axis=1)
    ot = ot * lax.rsqrt(ms + EPS) * gain * (1.0 - lambda_init)
    o_ref[0] = ot.T.astype(BF16)


def _attention(lam, q, k, vt, subln_g, bsz, seq, tq, tk, lambda_init):
    q3 = q.reshape(bsz, seq, D_ATTN)
    k3 = k.reshape(bsz, seq, D_ATTN)
    gain = jnp.broadcast_to(subln_g.astype(F32)[:, None], (HEAD_DIM_V, LANES))
    qmap = lambda b, h, i: (b, i, h)
    out = pl.pallas_call(
        functools.partial(_attn_body, tq=tq, tk=tk, lambda_init=lambda_init),
        grid=(bsz, N_HEADS, seq // tq),
        in_specs=[
            pl.BlockSpec(memory_space=pltpu.SMEM),
            pl.BlockSpec((1, tq, HEAD_DIM_V), qmap),
            pl.BlockSpec((1, seq, HEAD_DIM_V), lambda b, h, i: (b, 0, h)),
            pl.BlockSpec((HEAD_DIM_V, seq), lambda b, h, i: (h, b)),
            pl.BlockSpec((HEAD_DIM_V, LANES), lambda b, h, i: (0, 0)),
        ],
        out_specs=pl.BlockSpec((1, tq, HEAD_DIM_V), qmap),
        out_shape=jax.ShapeDtypeStruct((bsz, seq, D_ATTN), BF16),
        scratch_shapes=[pltpu.VMEM((2, HEAD_DIM_V, tq), F32)],
        compiler_params=_cparams(("arbitrary", "arbitrary", "arbitrary")),
        name="diff_attention",
    )(lam, q3, k3, vt, gain)
    return out.reshape(bsz * seq, D_ATTN)


def _out_router_body(h_ref, ys_ref, ya_ref, wo_ref, g_ref, wr_hi_ref, wr_lo_ref, br_ref,
                     h1_ref, xp_ref, idx_ref, gate_ref, rank_ref, cnt_ref, base_ref, *, tm):
    @pl.when(pl.program_id(0) == 0)
    def _():
        base_ref[...] = jnp.zeros_like(base_ref)

    h1 = (h_ref[...]
          + jnp.dot(ys_ref[...], wo_ref[:D_SSM, :], preferred_element_type=F32)
          + jnp.dot(ya_ref[...], wo_ref[D_SSM:, :], preferred_element_type=F32))
    h1_ref[...] = h1
    ms = jnp.mean(h1 * h1, axis=-1, keepdims=True)
    xn = h1 * lax.rsqrt(ms + EPS) * g_ref[...]
    _store_token_tiles(xp_ref, xn)

    x_hi = xn.astype(BF16)
    x_lo = (xn - x_hi.astype(F32)).astype(BF16)
    nt = (((1,), (1,)), ((), ()))
    logits = (lax.dot_general(wr_hi_ref[...], x_hi, nt, preferred_element_type=F32)
              + lax.dot_general(wr_hi_ref[...], x_lo, nt, preferred_element_type=F32)
              + lax.dot_general(wr_lo_ref[...], x_hi, nt, preferred_element_type=F32)
              + br_ref[...])

    e_iota = lax.broadcasted_iota(I32, logits.shape, 0)
    vals = logits
    top_v, top_i = [], []
    for _ in range(TOP_K):
        mx = jnp.max(vals, axis=0, keepdims=True)
        idx = jnp.min(jnp.where(vals == mx, e_iota, N_EXPERTS), axis=0, keepdims=True)
        top_v.append(mx)
        top_i.append(idx)
        vals = jnp.where(e_iota == idx, -jnp.inf, vals)
    ex = [jnp.exp(v - top_v[0]) for v in top_v]
    den = ex[0] + ex[1] + ex[2] + ex[3]
    gate_ref[...] = jnp.concatenate([e / den for e in ex], axis=0)
    idx_ref[...] = jnp.concatenate(top_i, axis=0)

    onehots = [e_iota == i for i in top_i]
    cnt = jnp.zeros(logits.shape, F32)
    for oh in onehots:
        cnt = cnt + oh.astype(F32)
    tri = (lax.broadcasted_iota(I32, (tm, tm), 0)
           <= lax.broadcasted_iota(I32, (tm, tm), 1)).astype(BF16)
    incl = jnp.dot(cnt.astype(BF16), tri, preferred_element_type=F32)
    base = base_ref[:, 0:1]
    excl = incl - cnt + base
    ranks = [jnp.sum(jnp.where(oh, excl, 0.0), axis=0, keepdims=True) for oh in onehots]
    rank_ref[...] = jnp.concatenate(ranks, axis=0).astype(I32)
    new_base = base + incl[:, tm - 1:tm]
    base_ref[...] = jnp.broadcast_to(new_base, base_ref.shape)
    cnt_ref[...] = jnp.broadcast_to(new_base, cnt_ref.shape)


def _out_router(h2, ys, ya, wo_bf, g, wr_hi, wr_lo, br, tm):
    n_tok = h2.shape[0]
    tok = lambda i: (i, 0)
    col = lambda i: (0, i)
    const = lambda i: (0, 0)
    return pl.pallas_call(
        functools.partial(_out_router_body, tm=tm),
        grid=(n_tok // tm,),
        in_specs=[
            pl.BlockSpec((tm, D_MODEL), tok),
            pl.BlockSpec((tm, D_SSM), tok),
            pl.BlockSpec((tm, D_ATTN), tok),
            pl.BlockSpec((D_MODEL, D_MODEL), const),
            pl.BlockSpec((1, D_MODEL), const),
            pl.BlockSpec((N_EXPERTS, D_MODEL), const),
            pl.BlockSpec((N_EXPERTS, D_MODEL), const),
            pl.BlockSpec((N_EXPERTS, 1), const),
        ],
        out_specs=[
            pl.BlockSpec((tm, D_MODEL), tok),
            pl.BlockSpec((tm * ROW_TILE, LANES), tok),
            pl.BlockSpec((TOP_K, tm), col),
            pl.BlockSpec((TOP_K, tm), col),
            pl.BlockSpec((TOP_K, tm), col),
            pl.BlockSpec((N_EXPERTS, LANES), const),
        ],
        out_shape=[
            jax.ShapeDtypeStruct((n_tok, D_MODEL), F32),
            jax.ShapeDtypeStruct((n_tok * ROW_TILE, LANES), F32),
            jax.ShapeDtypeStruct((TOP_K, n_tok), I32),
            jax.ShapeDtypeStruct((TOP_K, n_tok), F32),
            jax.ShapeDtypeStruct((TOP_K, n_tok), I32),
            jax.ShapeDtypeStruct((N_EXPERTS, LANES), F32),
        ],
        scratch_shapes=[pltpu.VMEM((N_EXPERTS, LANES), F32)],
        compiler_params=_cparams(("arbitrary",)),
        name="out_proj_router",
    )(h2, ys, ya, wo_bf, g, wr_hi, wr_lo, br)


def _row_tile(ref, r):
    return ref.at[pl.ds(pl.multiple_of(r * ROW_TILE, ROW_TILE), ROW_TILE), :]


def _dispatch_body(zs_ref, zf_ref, dest_ref, x_ref, xs_hbm, zero_ref, sem, zsem, *, tm, rows):
    @pl.when(pl.program_id(0) == 0)
    def _():
        zero_ref[...] = jnp.zeros_like(zero_ref)

        def fill(e):
            start = pl.multiple_of(zs_ref[e] * ROW_TILE, ROW_TILE)
            return pltpu.make_async_copy(zero_ref, xs_hbm.at[pl.ds(start, rows * ROW_TILE), :], zsem)

        for e in range(2 * N_EXPERTS):
            @pl.when(zf_ref[e] > 0)
            def _():
                fill(e).start()
        for e in range(2 * N_EXPERTS):
            @pl.when(zf_ref[e] > 0)
            def _():
                fill(e).wait()

    def issue(t, c):
        for k in range(TOP_K):
            d = dest_ref[0, 0, k * tm + t]
            pltpu.make_async_copy(_row_tile(x_ref, t), _row_tile(xs_hbm, d), sem).start(priority=k % 2)
        return c

    lax.fori_loop(0, tm, issue, 0, unroll=8)
    for k in range(TOP_K):
        pltpu.make_async_copy(x_ref, xs_hbm.at[pl.ds(0, tm * ROW_TILE), :], sem).wait()


def _dispatch(zstart, zflag, dest3, xp, n_rows, tm, rows):
    n_tok = xp.shape[0] // ROW_TILE
    return pl.pallas_call(
        functools.partial(_dispatch_body, tm=tm, rows=rows),
        grid_spec=pltpu.PrefetchScalarGridSpec(
            num_scalar_prefetch=2,
            grid=(n_tok // tm,),
            in_specs=[
                pl.BlockSpec((1, 1, TOP_K * tm), lambda i, zs, zf: (i, 0, 0),
                             memory_space=pltpu.SMEM),
                pl.BlockSpec((tm * ROW_TILE, LANES), lambda i, zs, zf: (i, 0)),
            ],
            out_specs=pl.BlockSpec(memory_space=pl.ANY),
            scratch_shapes=[
                pltpu.VMEM((rows * ROW_TILE, LANES), F32),
                pltpu.SemaphoreType.DMA,
                pltpu.SemaphoreType.DMA,
            ],
        ),
        out_shape=jax.ShapeDtypeStruct((n_rows * ROW_TILE, LANES), F32),
        compiler_params=_cparams(("arbitrary",)),
        name="moe_dispatch",
    )(zstart, zflag, dest3, xp)


def _expert_body(be_ref, nu_ref, xs_ref, wu_ref, bu_ref, wd_ref, bd_ref, y_ref,
                 wu_bf, wd_bf, *, rows):
    i = pl.program_id(0)
    prev = be_ref[jnp.maximum(i - 1, 0)]

    @pl.when((i == 0) | (be_ref[i] != prev))
    def _():
        wu_bf[...] = wu_ref[0, 0].astype(BF16)
        wd_bf[...] = wd_ref[0, 0].astype(BF16)

    @pl.when(i < nu_ref[0])
    def _():
        x = _load_token_tiles(xs_ref, rows).astype(BF16)
        h = jnp.dot(x, wu_bf[...], preferred_element_type=F32) + bu_ref[0, 0]
        glu = jnp.minimum(h[:, :D_FF], SWIGLU_LIMIT)
        lin = jnp.clip(h[:, D_FF:], -SWIGLU_LIMIT, SWIGLU_LIMIT)
        act = glu * jax.nn.sigmoid(SWIGLU_ALPHA * glu) * (lin + 1.0)
        y = jnp.dot(act.astype(BF16), wd_bf[...], preferred_element_type=F32) + bd_ref[0, 0]
        _store_token_tiles(y_ref, y)

    @pl.when(i >= nu_ref[0])
    def _():
        y_ref[...] = jnp.zeros_like(y_ref)


def _experts(block_expert, n_used, xs, w_up, b_up, w_down, b_down, layer, rows):
    n_blocks = xs.shape[0] // (rows * ROW_TILE)
    live = lambda i, be, nu: (jnp.minimum(i, nu[0] - 1), 0)
    wmap = lambda i, be, nu: (layer, be[i], 0, 0)
    return pl.pallas_call(
        functools.partial(_expert_body, rows=rows),
        grid_spec=pltpu.PrefetchScalarGridSpec(
            num_scalar_prefetch=2,
            grid=(n_blocks,),
            in_specs=[
                pl.BlockSpec((rows * ROW_TILE, LANES), live),
                pl.BlockSpec((1, 1, D_MODEL, 2 * D_FF), wmap),
                pl.BlockSpec((1, 1, 1, 2 * D_FF), wmap),
                pl.BlockSpec((1, 1, D_FF, D_MODEL), wmap),
                pl.BlockSpec((1, 1, 1, D_MODEL), wmap),
            ],
            out_specs=pl.BlockSpec((rows * ROW_TILE, LANES), lambda i, be, nu: (i, 0)),
            scratch_shapes=[
                pltpu.VMEM((D_MODEL, 2 * D_FF), BF16),
                pltpu.VMEM((D_FF, D_MODEL), BF16),
            ],
        ),
        out_shape=jax.ShapeDtypeStruct(xs.shape, F32),
        compiler_params=_cparams(("arbitrary",)),
        name="moe_experts",
    )(block_expert, n_used, xs, w_up, b_up, w_down, b_down)


def _combine_body(dest_ref, next_ref, h1_ref, gate_ref, y_hbm, o_ref, buf_ref, sems, *, tm):
    i = pl.program_id(0)
    slot = i % 2
    other = 1 - slot
    span = TOP_K * tm * ROW_TILE

    def slot_rows(s, first, n):
        return buf_ref.at[pl.ds(pl.multiple_of(s * span + first * ROW_TILE, ROW_TILE), n), :]

    def start_gathers(idx_ref, t, s):
        for k in range(TOP_K):
            d = idx_ref[0, 0, k * tm + t]
            pltpu.make_async_copy(_row_tile(y_hbm, d), slot_rows(s, k * tm + t, ROW_TILE),
                                  sems.at[s]).start(priority=k % 2)

    def wait_slot(s):
        pltpu.make_async_copy(y_hbm.at[pl.ds(0, span), :], slot_rows(s, 0, span), sems.at[s]).wait()

    @pl.when(i == 0)
    def _():
        def first(t, c):
            start_gathers(dest_ref, t, 0)
            return c
        lax.fori_loop(0, tm, first, 0, unroll=8)

    wait_slot(slot)

    def body(g, c):
        t0 = pl.multiple_of(g * SUBLANES, SUBLANES)
        gates = gate_ref[pl.ds(t0, SUBLANES), :]
        gate_cols = [gates[:, k:k + 1] for k in range(TOP_K)]
        for cc in range(ROW_TILE):
            lanes = slice(cc * LANES, (cc + 1) * LANES)
            acc = h1_ref[pl.ds(t0, SUBLANES), lanes]
            for k in range(TOP_K):
                first_row = slot * span + (k * tm + t0) * ROW_TILE + cc
                acc = acc + gate_cols[k] * buf_ref[pl.ds(first_row, SUBLANES, stride=ROW_TILE), :]
            o_ref[pl.ds(t0, SUBLANES), lanes] = acc
        for j in range(SUBLANES):
            start_gathers(next_ref, t0 + j, other)
        return c

    lax.fori_loop(0, tm // SUBLANES, body, 0)

    @pl.when(i == pl.num_programs(0) - 1)
    def _():
        wait_slot(other)


def _combine(dest3, h1, gates_t, y_buf, tm):
    n_tok = h1.shape[0]
    n_steps = n_tok // tm
    tok = lambda i: (i, 0)
    return pl.pallas_call(
        functools.partial(_combine_body, tm=tm),
        grid=(n_steps,),
        in_specs=[
            pl.BlockSpec((1, 1, TOP_K * tm), lambda i: (i, 0, 0), memory_space=pltpu.SMEM),
            pl.BlockSpec((1, 1, TOP_K * tm), lambda i: (jnp.minimum(i + 1, n_steps - 1), 0, 0),
                         memory_space=pltpu.SMEM),
            pl.BlockSpec((tm, D_MODEL), tok),
            pl.BlockSpec((tm, TOP_K), tok),
            pl.BlockSpec(memory_space=pl.ANY),
        ],
        out_specs=pl.BlockSpec((tm, D_MODEL), tok),
        out_shape=jax.ShapeDtypeStruct((n_tok, D_MODEL), F32),
        scratch_shapes=[
            pltpu.VMEM((2 * TOP_K * tm * ROW_TILE, LANES), F32),
            pltpu.SemaphoreType.DMA((2,)),
        ],
        compiler_params=_cparams(("arbitrary",)),
        name="moe_combine",
    )(dest3, dest3, h1, gates_t, y_buf)


def _tiles(seq, n_tok):
    tm = min(512, seq)
    ts = min(256, seq)
    tq = min(1024, seq)
    tk = min(1024, seq)
    tr = min(512, seq)
    rows = min(512, n_tok // 8)
    return tm, ts, tq, tk, tr, rows


def _rope_tables(seq):
    inv = 1.0 / (ROPE_THETA ** (jnp.arange(0, HEAD_DIM_QK, 2, dtype=F32) / HEAD_DIM_QK))
    ang = jnp.arange(seq, dtype=F32)[:, None] * inv[None, :]
    cos, sin = jnp.cos(ang), jnp.sin(ang)
    reps = LANES // HEAD_DIM_QK
    cos_t = jnp.concatenate([cos, cos] * reps, axis=1)
    sin_t = jnp.concatenate([-sin, sin] * reps, axis=1)
    return cos_t, sin_t


def kernel(x, g_mix, w_in, ssm_lam_re, ssm_lam_im, ssm_log_step, ssm_b_re, ssm_b_im, ssm_c_re, ssm_c_im, ssm_d, ssm_w_glu, ssm_norm_g, q_norm_g, k_norm_g, lam_q1, lam_k1, lam_q2, lam_k2, subln_g, w_out, g_ffn, w_router, b_router, w_up, b_up, w_down, b_down):
    bsz, seq, _ = x.shape
    n_tok = bsz * seq
    depth = w_in.shape[0]
    tm, ts, tq, tk, tr, rows = _tiles(seq, n_tok)
    n_blocks = (n_tok * TOP_K) // rows + N_EXPERTS
    n_rows = n_blocks * rows

    cos_t, sin_t = _rope_tables(seq)
    seg = jnp.kron(jnp.eye(D_ATTN // HEAD_DIM_QK, dtype=F32),
                   jnp.full((HEAD_DIM_QK, HEAD_DIM_QK), 1.0 / HEAD_DIM_QK, F32)).astype(BF16)

    h = x.reshape(n_tok, D_MODEL).astype(F32)
    for l in range(depth):
        lambda_init = 0.8 - 0.6 * math.exp(-0.3 * l)
        tile_g = lambda g: jnp.tile(g.astype(F32), D_ATTN // HEAD_DIM_QK)[None, :]
        w_uqk = w_in[l][:, :D_IN - D_ATTN].astype(BF16)
        w_vt = w_in[l][:, D_IN - D_ATTN:].T.astype(BF16)
        u, q, k, vt = _in_proj(h, g_mix[l][None, :].astype(F32), w_uqk, w_vt, cos_t, sin_t,
                               tile_g(q_norm_g[l]), tile_g(k_norm_g[l]), seg, seq, tm)

        bexp, cexp, lam_tab = _ssm_params(ssm_lam_re[l], ssm_lam_im[l], ssm_log_step[l],
                                          ssm_b_re[l], ssm_b_im[l], ssm_c_re[l], ssm_c_im[l])
        y_ssm = _ssm(u, bexp, cexp, lam_tab, ssm_d[l][None, :].astype(F32),
                     ssm_w_glu[l].astype(BF16), ssm_norm_g[l][None, :].astype(F32),
                     bsz, seq, ts)

        lam = (jnp.exp(jnp.sum(lam_q1[l].astype(F32) * lam_k1[l].astype(F32)))
               - jnp.exp(jnp.sum(lam_q2[l].astype(F32) * lam_k2[l].astype(F32)))
               + lambda_init).reshape(1).astype(F32)
        y_att = _attention(lam, q, k, vt, subln_g[l], bsz, seq, tq, tk, lambda_init)

        wr_t = w_router[l].astype(F32).T
        wr_hi = wr_t.astype(BF16)
        wr_lo = (wr_t - wr_hi.astype(F32)).astype(BF16)
        h1, xp, top_i, gates, rank, cnt = _out_router(
            h, y_ssm, y_att, w_out[l].astype(BF16), g_ffn[l][None, :].astype(F32),
            wr_hi, wr_lo, b_router[l].astype(F32)[:, None], tr)

        counts = cnt[:, 0].astype(I32)
        padded = (counts + rows - 1) // rows * rows
        pad_end = jnp.cumsum(padded)
        pad_start = pad_end - padded
        e_ids = jnp.arange(N_EXPERTS, dtype=I32)
        start_of = jnp.sum(jnp.where(top_i[None] == e_ids[:, None, None],
                                     pad_start[:, None, None], 0), axis=0)
        dest = start_of + rank
        dest3 = dest.reshape(TOP_K, n_tok // tr, tr).transpose(1, 0, 2).reshape(n_tok // tr, 1, TOP_K * tr)
        block_row = jnp.arange(n_blocks, dtype=I32) * rows
        block_expert = jnp.minimum(
            jnp.sum((pad_end[None, :] <= block_row[:, None]).astype(I32), axis=1),
            N_EXPERTS - 1).astype(I32)
        n_used = (pad_end[-1] // rows).reshape(1).astype(I32)
        spare = n_blocks - 1 - jnp.arange(N_EXPERTS, dtype=I32)
        zstart = jnp.concatenate([jnp.maximum(pad_end - rows, 0), spare * rows]).astype(I32)
        zflag = jnp.concatenate([padded > 0, spare >= n_used[0]]).astype(I32)

        xs = _dispatch(zstart, zflag, dest3, xp, n_rows, tr, rows)
        y_buf = _experts(block_expert, n_used, xs, w_up, b_up[:, :, None, :],
                         w_down, b_down[:, :, None, :], l, rows)
        h = _combine(dest3, h1, gates.T, y_buf, tr)
    return h.reshape(bsz, seq, D_MODEL).astype(x.dtype)
```

```python
import functools
import math

import jax
import jax.numpy as jnp
from jax import lax
from jax.experimental import pallas as pl
from jax.experimental.pallas import tpu as pltpu

F32 = jnp.float32
BF16 = jnp.bfloat16
I32 = jnp.int32

D_MODEL = 1024
CHUNK = 64
D_SSM = 512
SSM_GROUP = 16
N_GROUPS = 32
SSM_STATE = 64
D_ATTN = 512
N_HEADS = 4
HEAD_DIM_V = 128
HEAD_DIM_QK = 64
D_IN = D_SSM + 3 * D_ATTN
ROPE_THETA = 10000.0
N_EXPERTS = 32
TOP_K = 4
D_FF = 1024
SWIGLU_LIMIT = 7.0
SWIGLU_ALPHA = 1.702
EPS = 1e-6

LANES = 128
SUBLANES = 8
MXU_TILE = 256
STATE_LANES = 2 * N_GROUPS * SSM_STATE
N_STATE_TILES = STATE_LANES // MXU_TILE
ROW_TILE = D_MODEL // LANES
VMEM_LIMIT = 56 * 1024 * 1024


def _cparams(sem, vmem=VMEM_LIMIT):
    return pltpu.CompilerParams(dimension_semantics=sem, vmem_limit_bytes=vmem)


def _store_token_tiles(ref, val, base=0):
    n = val.shape[0]
    for c in range(ROW_TILE):
        ref[pl.ds(base + c, n, stride=ROW_TILE), :] = val[:, c * LANES:(c + 1) * LANES]


def _load_token_tiles(ref, n, base=0):
    return jnp.concatenate(
        [ref[pl.ds(base + c, n, stride=ROW_TILE), :] for c in range(ROW_TILE)], axis=1)


def _in_proj_body(x_ref, g_ref, w_ref, wvt_ref, cos_ref, sin_ref, qg_ref, kg_ref, seg_ref,
                  u_ref, q_ref, k_ref, vt_ref):
    x = x_ref[...]
    ms = jnp.mean(x * x, axis=-1, keepdims=True)
    xn = (x * lax.rsqrt(ms + EPS) * g_ref[...]).astype(BF16)
    proj = jnp.dot(xn, w_ref[...], preferred_element_type=F32)
    u_ref[...] = proj[:, :D_SSM].astype(BF16)
    vt_ref[...] = lax.dot_general(wvt_ref[...], xn, (((1,), (1,)), ((), ())),
                                  preferred_element_type=F32).astype(BF16)

    reps = D_ATTN // LANES
    cos = jnp.concatenate([cos_ref[...]] * reps, axis=1)
    sin = jnp.concatenate([sin_ref[...]] * reps, axis=1)
    lane = lax.broadcasted_iota(I32, cos.shape, 1)
    first_half = (lane & (HEAD_DIM_QK // 2)) == 0

    def prep(z, gain):
        ms = jnp.dot((z * z).astype(BF16), seg_ref[...], preferred_element_type=F32)
        zn = z * lax.rsqrt(ms + EPS) * gain
        half = HEAD_DIM_QK // 2
        swapped = jnp.where(first_half, pltpu.roll(zn, D_ATTN - half, 1), pltpu.roll(zn, half, 1))
        return zn * cos + swapped * sin

    scale = math.log2(math.e) / math.sqrt(HEAD_DIM_QK)
    q_ref[...] = (prep(proj[:, D_SSM:D_SSM + D_ATTN], qg_ref[...]) * scale).astype(BF16)
    k_ref[...] = prep(proj[:, D_SSM + D_ATTN:D_SSM + 2 * D_ATTN], kg_ref[...]).astype(BF16)


def _in_proj(x2, g, w_bf, wvt_bf, cos_t, sin_t, qg, kg, seg, seq, tm):
    n_tok = x2.shape[0]
    n_pos = seq // tm
    tok = lambda i: (i, 0)
    const = lambda i: (0, 0)
    pos = lambda i: (i % n_pos, 0)
    out = jax.ShapeDtypeStruct((n_tok, D_SSM), BF16)
    out_t = jax.ShapeDtypeStruct((D_ATTN, n_tok), BF16)
    return pl.pallas_call(
        _in_proj_body,
        grid=(n_tok // tm,),
        in_specs=[
            pl.BlockSpec((tm, D_MODEL), tok),
            pl.BlockSpec((1, D_MODEL), const),
            pl.BlockSpec((D_MODEL, D_IN - D_ATTN), const),
            pl.BlockSpec((D_ATTN, D_MODEL), const),
            pl.BlockSpec((tm, LANES), pos),
            pl.BlockSpec((tm, LANES), pos),
            pl.BlockSpec((1, D_ATTN), const),
            pl.BlockSpec((1, D_ATTN), const),
            pl.BlockSpec((D_ATTN, D_ATTN), const),
        ],
        out_specs=[pl.BlockSpec((tm, D_SSM), tok)] * 3 + [pl.BlockSpec((D_ATTN, tm), lambda i: (0, i))],
        out_shape=[out] * 3 + [out_t],
        compiler_params=_cparams(("arbitrary",)),
        name="in_proj",
    )(x2, g, w_bf, wvt_bf, cos_t, sin_t, qg, kg, seg)


FOLD_TILES = STATE_LANES // LANES
FOLD_PITCH = 40


def _ssm_body(u_ref, bexp_ref, cexp_ref, lam_ref, d_ref, wglu_ref, gn_ref, o_ref,
              fold_ref, state_ref, *, tm, nb):
    @pl.when(pl.program_id(0) == 0)
    def _():
        state_ref[...] = jnp.zeros_like(state_ref)

    half = FOLD_TILES // 2
    slab = half // 2
    chunk = 4 * LANES

    def fold_rows(b, tile):
        return pl.ds(b * tm * FOLD_PITCH + tile, tm, stride=FOLD_PITCH)

    def slab_tiles(kt):
        return ([kt * slab + j for j in range(slab)]
                + [half + kt * slab + j for j in range(slab)])

    for b in range(nb):
        for kt in range(2):
            uk = u_ref[b, :, kt * MXU_TILE:(kt + 1) * MXU_TILE]
            tiles = slab_tiles(kt)
            for q in range(len(tiles) * LANES // chunk):
                res = jnp.dot(uk, bexp_ref[kt, :, q * chunk:(q + 1) * chunk],
                              preferred_element_type=F32)
                for i in range(chunk // LANES):
                    fold_ref[fold_rows(b, tiles[q * (chunk // LANES) + i]), :] = (
                        res[:, i * LANES:(i + 1) * LANES])

    lam_r = lam_ref[0:half, :]
    lam_i = lam_ref[half:FOLD_TILES, :]

    def body(t, carry):
        new = []
        for b in range(nb):
            xr, xi = carry[b]
            base = pl.multiple_of((b * tm + t) * FOLD_PITCH, SUBLANES)
            nr = lam_r * xr - lam_i * xi + fold_ref[pl.ds(base, half), :]
            ni = lam_r * xi + lam_i * xr + fold_ref[pl.ds(base + half, half), :]
            fold_ref[pl.ds(base, half), :] = nr
            fold_ref[pl.ds(base + half, half), :] = ni
            new.append((nr, ni))
        return tuple(new)

    init = tuple((state_ref[b, 0:half, :], state_ref[b, half:FOLD_TILES, :]) for b in range(nb))
    final = lax.fori_loop(0, tm, body, init, unroll=2)
    for b in range(nb):
        state_ref[b, 0:half, :] = final[b][0]
        state_ref[b, half:FOLD_TILES, :] = final[b][1]

    for b in range(nb):
        parts = []
        for n in range(2):
            xs = jnp.concatenate([fold_ref[fold_rows(b, tile), :] for tile in slab_tiles(n)],
                                 axis=1).astype(BF16)
            parts.append(jnp.dot(xs, cexp_ref[n], preferred_element_type=F32))
        y = jnp.concatenate(parts, axis=1) + d_ref[...] * u_ref[b].astype(F32)
        y = jax.nn.gelu(y)
        y = y * jax.nn.sigmoid(jnp.dot(y.astype(BF16), wglu_ref[...], preferred_element_type=F32))
        ms = jnp.mean(y * y, axis=-1, keepdims=True)
        o_ref[b] = (y * lax.rsqrt(ms + EPS) * gn_ref[...]).astype(BF16)


def _ssm(u, bexp, cexp, lam_tab, d, wglu_bf, gn, bsz, seq, tm):
    u3 = u.reshape(bsz, seq, D_SSM)
    tok = lambda s: (0, s, 0)
    c2 = lambda s: (0, 0)
    c3 = lambda s: (0, 0, 0)
    out = pl.pallas_call(
        functools.partial(_ssm_body, tm=tm, nb=bsz),
        grid=(seq // tm,),
        in_specs=[
            pl.BlockSpec((bsz, tm, D_SSM), tok),
            pl.BlockSpec((2, MXU_TILE, STATE_LANES // 2), c3),
            pl.BlockSpec((2, STATE_LANES // 2, MXU_TILE), c3),
            pl.BlockSpec((FOLD_TILES, LANES), c2),
            pl.BlockSpec((1, D_SSM), c2),
            pl.BlockSpec((D_SSM, D_SSM), c2),
            pl.BlockSpec((1, D_SSM), c2),
        ],
        out_specs=pl.BlockSpec((bsz, tm, D_SSM), tok),
        out_shape=jax.ShapeDtypeStruct((bsz, seq, D_SSM), BF16),
        scratch_shapes=[
            pltpu.VMEM((bsz * tm * FOLD_PITCH, LANES), F32),
            pltpu.VMEM((bsz, FOLD_TILES, LANES), F32),
        ],
        compiler_params=_cparams(("arbitrary",)),
        name="ssm_mixer",
    )(u3, bexp, cexp, lam_tab, d, wglu_bf, gn)
    return out.reshape(bsz * seq, D_SSM)


def _ssm_params(lam_re, lam_im, log_step, b_re, b_im, c_re, c_im):
    g_n = N_GROUPS
    step = jnp.exp(log_step.astype(F32))[:, None]
    lam_re = lam_re.astype(F32)
    lam_im = lam_im.astype(F32)
    mag = jnp.exp(lam_re * step)
    lbr = mag * jnp.cos(lam_im * step)
    lbi = mag * jnp.sin(lam_im * step)
    nr, ni = lbr - 1.0, lbi
    den = lam_re * lam_re + lam_im * lam_im
    fr = (nr * lam_re + ni * lam_im) / den
    fi = (ni * lam_re - nr * lam_im) / den
    b_re = b_re.astype(F32)
    b_im = b_im.astype(F32)
    bbr = fr[..., None] * b_re - fi[..., None] * b_im
    bbi = fr[..., None] * b_im + fi[..., None] * b_re

    eye = jnp.eye(g_n, dtype=F32)
    bb = jnp.stack([bbr, bbi], axis=0)
    dense_b = jnp.einsum('cgph,kg->khcgp', bb, eye).reshape(D_SSM, STATE_LANES)
    cc = jnp.stack([c_re.astype(F32), -c_im.astype(F32)], axis=0)
    dense_c = jnp.einsum('cghp,kg->cgpkh', cc, eye).reshape(STATE_LANES, D_SSM)

    part = STATE_LANES // 2
    span = part // 2
    bexp = jnp.stack([
        jnp.concatenate([dense_b[kt * MXU_TILE:(kt + 1) * MXU_TILE,
                                 c * part + kt * span:c * part + (kt + 1) * span]
                         for c in range(2)], axis=1) for kt in range(2)])
    cexp = jnp.stack([
        jnp.concatenate([dense_c[c * part + kt * span:c * part + (kt + 1) * span,
                                 kt * MXU_TILE:(kt + 1) * MXU_TILE]
                         for c in range(2)], axis=0) for kt in range(2)])
    lam_tab = jnp.concatenate([lbr.reshape(FOLD_TILES // 2, LANES),
                               lbi.reshape(FOLD_TILES // 2, LANES)], axis=0)
    return bexp.astype(BF16), cexp.astype(BF16), lam_tab.astype(F32)


MASK_PENALTY = -1e30


def _attn_body(lam_ref, q_ref, k_ref, vt_ref, chunk_ref, g_ref, o_ref, s_ref, acc_ref, st_ref,
               *, tq, lambda_init):
    qi = pl.program_id(2)
    n_chunk = tq // CHUNK
    qt = q_ref[0].astype(F32).T.astype(BF16)
    row = lax.broadcasted_iota(I32, qt.shape, 0)
    zero = jnp.zeros_like(qt)
    qt_maps = (jnp.where(row < HEAD_DIM_QK, qt, zero), jnp.where(row >= HEAD_DIM_QK, qt, zero))
    qry_chunk = lax.broadcasted_iota(I32, qt.shape, 1) // CHUNK
    penalty = jnp.where((row < n_chunk) & (qry_chunk < row), MASK_PENALTY, 0.0).astype(BF16)

    acc_ref[...] = jnp.zeros(acc_ref.shape, F32)
    st_ref[...] = jnp.zeros(st_ref.shape, F32)
    for m in range(2):
        st_ref[pl.ds(2 * m, 1), :] = jnp.full((1, tq), -jnp.inf, F32)

    def scores(ki, slot):
        start = pl.multiple_of(ki * tq, tq)
        kb = jnp.concatenate([k_ref[0, pl.ds(start, tq), :], chunk_ref[...]], axis=1)
        pen = jnp.where(ki == qi, penalty, zero)
        for m in range(2):
            s = jnp.dot(kb, jnp.concatenate([qt_maps[m], pen], axis=0),
                        preferred_element_type=F32)
            s_ref[slot, m] = s
            st_ref[pl.ds(4 + 2 * slot + m, 1), :] = jnp.max(s, axis=0, keepdims=True)

    def absorb(ki, slot):
        start = pl.multiple_of(ki * tq, tq)
        vtb = vt_ref[:, pl.ds(start, tq)]
        for m in range(2):
            m_old = st_ref[pl.ds(2 * m, 1), :]
            m_new = jnp.maximum(m_old, st_ref[pl.ds(4 + 2 * slot + m, 1), :])
            p = jnp.exp2(s_ref[slot, m] - m_new)
            alpha = jnp.exp2(m_old - m_new)
            st_ref[pl.ds(2 * m + 1, 1), :] = (alpha * st_ref[pl.ds(2 * m + 1, 1), :]
                                              + jnp.sum(p, axis=0, keepdims=True))
            acc_ref[m] = alpha * acc_ref[m] + jnp.dot(vtb, p.astype(BF16),
                                                      preferred_element_type=F32)
            st_ref[pl.ds(2 * m, 1), :] = m_new

    scores(0, 0)

    def two_blocks(pair, c):
        j = 2 * pair
        scores(j + 1, 1)
        absorb(j, 0)
        scores(j + 2, 0)
        absorb(j + 1, 1)
        return c

    lax.fori_loop(0, qi // 2, two_blocks, 0)

    @pl.when(qi % 2 == 1)
    def _():
        absorb(qi - 1, 0)
        scores(qi, 1)
        absorb(qi, 1)

    @pl.when(qi % 2 == 0)
    def _():
        absorb(qi, 0)

    l0 = st_ref[pl.ds(1, 1), :]
    l1 = st_ref[pl.ds(3, 1), :]
    ot = acc_ref[0] / l0 - lam_ref[0] * (acc_ref[1] / l1)
    ms = jnp.mean(ot * ot, axis=0, keepdims=True)
    gain = jnp.concatenate([g_ref[...]] * (tq // LANES), axis=1)
    ot = ot * lax.rsqrt(ms + EPS) * gain * (1.0 - lambda_init)
    o_ref[0] = ot.T.astype(BF16)


def _attention(lam, q, k, vt, subln_g, bsz, seq, tq, lambda_init):
    q3 = q.reshape(bsz, seq, D_ATTN)
    k3 = k.reshape(bsz, seq, D_ATTN)
    gain = jnp.broadcast_to(subln_g.astype(F32)[:, None], (HEAD_DIM_V, LANES))
    key_chunk = jnp.arange(tq, dtype=I32)[:, None] // CHUNK
    chunk_cols = (key_chunk == jnp.arange(LANES, dtype=I32)[None, :]).astype(BF16)
    qmap = lambda b, h, i: (b, i, h)
    const = lambda b, h, i: (0, 0)
    out = pl.pallas_call(
        functools.partial(_attn_body, tq=tq, lambda_init=lambda_init),
        grid=(bsz, N_HEADS, seq // tq),
        in_specs=[
            pl.BlockSpec(memory_space=pltpu.SMEM),
            pl.BlockSpec((1, tq, HEAD_DIM_V), qmap),
            pl.BlockSpec((1, seq, HEAD_DIM_V), lambda b, h, i: (b, 0, h)),
            pl.BlockSpec((HEAD_DIM_V, seq), lambda b, h, i: (h, b)),
            pl.BlockSpec((tq, LANES), const),
            pl.BlockSpec((HEAD_DIM_V, LANES), const),
        ],
        out_specs=pl.BlockSpec((1, tq, HEAD_DIM_V), qmap),
        out_shape=jax.ShapeDtypeStruct((bsz, seq, D_ATTN), BF16),
        scratch_shapes=[
            pltpu.VMEM((2, 2, tq, tq), F32),
            pltpu.VMEM((2, HEAD_DIM_V, tq), F32),
            pltpu.VMEM((SUBLANES, tq), F32),
        ],
        compiler_params=_cparams(("arbitrary", "arbitrary", "arbitrary")),
        name="diff_attention",
    )(lam, q3, k3, vt, chunk_cols, gain)
    return out.reshape(bsz * seq, D_ATTN)


def _out_router_body(h_ref, ys_ref, ya_ref, wo_ref, g_ref, wr_hi_ref, wr_lo_ref, br_ref,
                     h1_ref, xp_ref, idx_ref, gate_ref, rank_ref, cnt_ref, base_ref, *, tm):
    @pl.when(pl.program_id(0) == 0)
    def _():
        base_ref[...] = jnp.zeros_like(base_ref)

    h1 = (h_ref[...]
          + jnp.dot(ys_ref[...], wo_ref[:D_SSM, :], preferred_element_type=F32)
          + jnp.dot(ya_ref[...], wo_ref[D_SSM:, :], preferred_element_type=F32))
    h1_ref[...] = h1
    ms = jnp.mean(h1 * h1, axis=-1, keepdims=True)
    xn = h1 * lax.rsqrt(ms + EPS) * g_ref[...]
    _store_token_tiles(xp_ref, xn)

    x_hi = xn.astype(BF16)
    x_lo = (xn - x_hi.astype(F32)).astype(BF16)
    nt = (((1,), (1,)), ((), ()))
    logits = (lax.dot_general(wr_hi_ref[...], x_hi, nt, preferred_element_type=F32)
              + lax.dot_general(wr_hi_ref[...], x_lo, nt, preferred_element_type=F32)
              + lax.dot_general(wr_lo_ref[...], x_hi, nt, preferred_element_type=F32)
              + br_ref[...])

    e_iota = lax.broadcasted_iota(I32, logits.shape, 0)
    vals = logits
    top_v, top_i = [], []
    for _ in range(TOP_K):
        mx = jnp.max(vals, axis=0, keepdims=True)
        idx = jnp.min(jnp.where(vals == mx, e_iota, N_EXPERTS), axis=0, keepdims=True)
        top_v.append(mx)
        top_i.append(idx)
        vals = jnp.where(e_iota == idx, -jnp.inf, vals)
    ex = [jnp.exp(v - top_v[0]) for v in top_v]
    den = ex[0] + ex[1] + ex[2] + ex[3]
    gate_ref[...] = jnp.concatenate([e / den for e in ex], axis=0)
    idx_ref[...] = jnp.concatenate(top_i, axis=0)

    onehots = [e_iota == i for i in top_i]
    cnt = jnp.zeros(logits.shape, F32)
    for oh in onehots:
        cnt = cnt + oh.astype(F32)
    tri = (lax.broadcasted_iota(I32, (tm, tm), 0)
           <= lax.broadcasted_iota(I32, (tm, tm), 1)).astype(BF16)
    incl = jnp.dot(cnt.astype(BF16), tri, preferred_element_type=F32)
    base = base_ref[:, 0:1]
    excl = incl - cnt + base
    ranks = [jnp.sum(jnp.where(oh, excl, 0.0), axis=0, keepdims=True) for oh in onehots]
    rank_ref[...] = jnp.concatenate(ranks, axis=0).astype(I32)
    new_base = base + incl[:, tm - 1:tm]
    base_ref[...] = jnp.broadcast_to(new_base, base_ref.shape)
    cnt_ref[...] = jnp.broadcast_to(new_base, cnt_ref.shape)


def _out_router(h2, ys, ya, wo_bf, g, wr_hi, wr_lo, br, tm):
    n_tok = h2.shape[0]
    tok = lambda i: (i, 0)
    col = lambda i: (0, i)
    const = lambda i: (0, 0)
    return pl.pallas_call(
        functools.partial(_out_router_body, tm=tm),
        grid=(n_tok // tm,),
        in_specs=[
            pl.BlockSpec((tm, D_MODEL), tok),
            pl.BlockSpec((tm, D_SSM), tok),
            pl.BlockSpec((tm, D_ATTN), tok),
            pl.BlockSpec((D_MODEL, D_MODEL), const),
            pl.BlockSpec((1, D_MODEL), const),
            pl.BlockSpec((N_EXPERTS, D_MODEL), const),
            pl.BlockSpec((N_EXPERTS, D_MODEL), const),
            pl.BlockSpec((N_EXPERTS, 1), const),
        ],
        out_specs=[
            pl.BlockSpec((tm, D_MODEL), tok),
            pl.BlockSpec((tm * ROW_TILE, LANES), tok),
            pl.BlockSpec((TOP_K, tm), col),
            pl.BlockSpec((TOP_K, tm), col),
            pl.BlockSpec((TOP_K, tm), col),
            pl.BlockSpec((N_EXPERTS, LANES), const),
        ],
        out_shape=[
            jax.ShapeDtypeStruct((n_tok, D_MODEL), F32),
            jax.ShapeDtypeStruct((n_tok * ROW_TILE, LANES), F32),
            jax.ShapeDtypeStruct((TOP_K, n_tok), I32),
            jax.ShapeDtypeStruct((TOP_K, n_tok), F32),
            jax.ShapeDtypeStruct((TOP_K, n_tok), I32),
            jax.ShapeDtypeStruct((N_EXPERTS, LANES), F32),
        ],
        scratch_shapes=[pltpu.VMEM((N_EXPERTS, LANES), F32)],
        compiler_params=_cparams(("arbitrary",)),
        name="out_proj_router",
    )(h2, ys, ya, wo_bf, g, wr_hi, wr_lo, br)


def _row_tile(ref, r):
    return ref.at[pl.ds(pl.multiple_of(r * ROW_TILE, ROW_TILE), ROW_TILE), :]


def _dispatch_body(zs_ref, zf_ref, dest_ref, x_ref, xs_hbm, zero_ref, sem, zsem, *, tm, rows):
    @pl.when(pl.program_id(0) == 0)
    def _():
        zero_ref[...] = jnp.zeros_like(zero_ref)

        def fill(e):
            start = pl.multiple_of(zs_ref[e] * ROW_TILE, ROW_TILE)
            return pltpu.make_async_copy(zero_ref, xs_hbm.at[pl.ds(start, rows * ROW_TILE), :], zsem)

        for e in range(2 * N_EXPERTS):
            @pl.when(zf_ref[e] > 0)
            def _():
                fill(e).start()
        for e in range(2 * N_EXPERTS):
            @pl.when(zf_ref[e] > 0)
            def _():
                fill(e).wait()

    def issue(t, c):
        for k in range(TOP_K):
            d = dest_ref[0, 0, k * tm + t]
            pltpu.make_async_copy(_row_tile(x_ref, t), _row_tile(xs_hbm, d), sem).start(priority=k % 2)
        return c

    lax.fori_loop(0, tm, issue, 0, unroll=8)
    for k in range(TOP_K):
        pltpu.make_async_copy(x_ref, xs_hbm.at[pl.ds(0, tm * ROW_TILE), :], sem).wait()


def _dispatch(zstart, zflag, dest3, xp, n_rows, tm, rows):
    n_tok = xp.shape[0] // ROW_TILE
    return pl.pallas_call(
        functools.partial(_dispatch_body, tm=tm, rows=rows),
        grid_spec=pltpu.PrefetchScalarGridSpec(
            num_scalar_prefetch=2,
            grid=(n_tok // tm,),
            in_specs=[
                pl.BlockSpec((1, 1, TOP_K * tm), lambda i, zs, zf: (i, 0, 0),
                             memory_space=pltpu.SMEM),
                pl.BlockSpec((tm * ROW_TILE, LANES), lambda i, zs, zf: (i, 0)),
            ],
            out_specs=pl.BlockSpec(memory_space=pl.ANY),
            scratch_shapes=[
                pltpu.VMEM((rows * ROW_TILE, LANES), F32),
                pltpu.SemaphoreType.DMA,
                pltpu.SemaphoreType.DMA,
            ],
        ),
        out_shape=jax.ShapeDtypeStruct((n_rows * ROW_TILE, LANES), F32),
        compiler_params=_cparams(("arbitrary",)),
        name="moe_dispatch",
    )(zstart, zflag, dest3, xp)


def _expert_body(be_ref, nu_ref, xs_ref, wu_ref, bu_ref, wd_ref, bd_ref, y_ref,
                 wu_bf, wd_bf, *, rows):
    i = pl.program_id(0)
    prev = be_ref[jnp.maximum(i - 1, 0)]

    @pl.when((i == 0) | (be_ref[i] != prev))
    def _():
        wu_bf[...] = wu_ref[0, 0].astype(BF16)
        wd_bf[...] = wd_ref[0, 0].astype(BF16)

    @pl.when(i < nu_ref[0])
    def _():
        x = _load_token_tiles(xs_ref, rows).astype(BF16)
        h = jnp.dot(x, wu_bf[...], preferred_element_type=F32) + bu_ref[0, 0]
        glu = jnp.minimum(h[:, :D_FF], SWIGLU_LIMIT)
        lin = jnp.clip(h[:, D_FF:], -SWIGLU_LIMIT, SWIGLU_LIMIT)
        act = glu * jax.nn.sigmoid(SWIGLU_ALPHA * glu) * (lin + 1.0)
        y = jnp.dot(act.astype(BF16), wd_bf[...], preferred_element_type=F32) + bd_ref[0, 0]
        _store_token_tiles(y_ref, y)

    @pl.when(i >= nu_ref[0])
    def _():
        y_ref[...] = jnp.zeros_like(y_ref)


def _experts(block_expert, n_used, xs, w_up, b_up, w_down, b_down, layer, rows):
    n_blocks = xs.shape[0] // (rows * ROW_TILE)
    live = lambda i, be, nu: (jnp.minimum(i, nu[0] - 1), 0)
    wmap = lambda i, be, nu: (layer, be[i], 0, 0)
    return pl.pallas_call(
        functools.partial(_expert_body, rows=rows),
        grid_spec=pltpu.PrefetchScalarGridSpec(
            num_scalar_prefetch=2,
            grid=(n_blocks,),
            in_specs=[
                pl.BlockSpec((rows * ROW_TILE, LANES), live),
                pl.BlockSpec((1, 1, D_MODEL, 2 * D_FF), wmap),
                pl.BlockSpec((1, 1, 1, 2 * D_FF), wmap),
                pl.BlockSpec((1, 1, D_FF, D_MODEL), wmap),
                pl.BlockSpec((1, 1, 1, D_MODEL), wmap),
            ],
            out_specs=pl.BlockSpec((rows * ROW_TILE, LANES), lambda i, be, nu: (i, 0)),
            scratch_shapes=[
                pltpu.VMEM((D_MODEL, 2 * D_FF), BF16),
                pltpu.VMEM((D_FF, D_MODEL), BF16),
            ],
        ),
        out_shape=jax.ShapeDtypeStruct(xs.shape, F32),
        compiler_params=_cparams(("arbitrary",)),
        name="moe_experts",
    )(block_expert, n_used, xs, w_up, b_up, w_down, b_down)


def _combine_body(dest_ref, next_ref, h1_ref, gate_ref, y_hbm, o_ref, buf_ref, sems, *, tm):
    i = pl.program_id(0)
    slot = i % 2
    other = 1 - slot
    span = TOP_K * tm * ROW_TILE

    def slot_rows(s, first, n):
        return buf_ref.at[pl.ds(pl.multiple_of(s * span + first * ROW_TILE, ROW_TILE), n), :]

    def start_gathers(idx_ref, t, s):
        for k in range(TOP_K):
            d = idx_ref[0, 0, k * tm + t]
            pltpu.make_async_copy(_row_tile(y_hbm, d), slot_rows(s, k * tm + t, ROW_TILE),
                                  sems.at[s]).start(priority=k % 2)

    def wait_slot(s):
        pltpu.make_async_copy(y_hbm.at[pl.ds(0, span), :], slot_rows(s, 0, span), sems.at[s]).wait()

    @pl.when(i == 0)
    def _():
        def first(t, c):
            start_gathers(dest_ref, t, 0)
            return c
        lax.fori_loop(0, tm, first, 0, unroll=8)

    wait_slot(slot)

    def body(g, c):
        t0 = pl.multiple_of(g * SUBLANES, SUBLANES)
        gates = gate_ref[pl.ds(t0, SUBLANES), :]
        gate_cols = [gates[:, k:k + 1] for k in range(TOP_K)]
        for cc in range(ROW_TILE):
            lanes = slice(cc * LANES, (cc + 1) * LANES)
            acc = h1_ref[pl.ds(t0, SUBLANES), lanes]
            for k in range(TOP_K):
                first_row = slot * span + (k * tm + t0) * ROW_TILE + cc
                acc = acc + gate_cols[k] * buf_ref[pl.ds(first_row, SUBLANES, stride=ROW_TILE), :]
            o_ref[pl.ds(t0, SUBLANES), lanes] = acc
        for j in range(SUBLANES):
            start_gathers(next_ref, t0 + j, other)
        return c

    lax.fori_loop(0, tm // SUBLANES, body, 0)

    @pl.when(i == pl.num_programs(0) - 1)
    def _():
        wait_slot(other)


def _combine(dest3, h1, gates_t, y_buf, tm):
    n_tok = h1.shape[0]
    n_steps = n_tok // tm
    tok = lambda i: (i, 0)
    return pl.pallas_call(
        functools.partial(_combine_body, tm=tm),
        grid=(n_steps,),
        in_specs=[
            pl.BlockSpec((1, 1, TOP_K * tm), lambda i: (i, 0, 0), memory_space=pltpu.SMEM),
            pl.BlockSpec((1, 1, TOP_K * tm), lambda i: (jnp.minimum(i + 1, n_steps - 1), 0, 0),
                         memory_space=pltpu.SMEM),
            pl.BlockSpec((tm, D_MODEL), tok),
            pl.BlockSpec((tm, TOP_K), tok),
            pl.BlockSpec(memory_space=pl.ANY),
        ],
        out_specs=pl.BlockSpec((tm, D_MODEL), tok),
        out_shape=jax.ShapeDtypeStruct((n_tok, D_MODEL), F32),
        scratch_shapes=[
            pltpu.VMEM((2 * TOP_K * tm * ROW_TILE, LANES), F32),
            pltpu.SemaphoreType.DMA((2,)),
        ],
        compiler_params=_cparams(("arbitrary",)),
        name="moe_combine",
    )(dest3, dest3, h1, gates_t, y_buf)


def _tiles(seq, n_tok):
    tm = min(512, seq)
    ts = min(256, seq)
    tq = min(512, seq)
    tr = min(512, seq)
    rows = min(512, n_tok // 8)
    return tm, ts, tq, tr, rows


def _rope_tables(seq):
    inv = 1.0 / (ROPE_THETA ** (jnp.arange(0, HEAD_DIM_QK, 2, dtype=F32) / HEAD_DIM_QK))
    ang = jnp.arange(seq, dtype=F32)[:, None] * inv[None, :]
    cos, sin = jnp.cos(ang), jnp.sin(ang)
    reps = LANES // HEAD_DIM_QK
    cos_t = jnp.concatenate([cos, cos] * reps, axis=1)
    sin_t = jnp.concatenate([-sin, sin] * reps, axis=1)
    return cos_t, sin_t


def kernel(x, g_mix, w_in, ssm_lam_re, ssm_lam_im, ssm_log_step, ssm_b_re, ssm_b_im, ssm_c_re, ssm_c_im, ssm_d, ssm_w_glu, ssm_norm_g, q_norm_g, k_norm_g, lam_q1, lam_k1, lam_q2, lam_k2, subln_g, w_out, g_ffn, w_router, b_router, w_up, b_up, w_down, b_down):
    bsz, seq, _ = x.shape
    n_tok = bsz * seq
    depth = w_in.shape[0]
    tm, ts, tq, tr, rows = _tiles(seq, n_tok)
    n_blocks = (n_tok * TOP_K) // rows + N_EXPERTS
    n_rows = n_blocks * rows

    cos_t, sin_t = _rope_tables(seq)
    seg = jnp.kron(jnp.eye(D_ATTN // HEAD_DIM_QK, dtype=F32),
                   jnp.full((HEAD_DIM_QK, HEAD_DIM_QK), 1.0 / HEAD_DIM_QK, F32)).astype(BF16)

    h = x.reshape(n_tok, D_MODEL).astype(F32)
    for l in range(depth):
        lambda_init = 0.8 - 0.6 * math.exp(-0.3 * l)
        tile_g = lambda g: jnp.tile(g.astype(F32), D_ATTN // HEAD_DIM_QK)[None, :]
        w_uqk = w_in[l][:, :D_IN - D_ATTN].astype(BF16)
        w_vt = w_in[l][:, D_IN - D_ATTN:].T.astype(BF16)
        u, q, k, vt = _in_proj(h, g_mix[l][None, :].astype(F32), w_uqk, w_vt, cos_t, sin_t,
                               tile_g(q_norm_g[l]), tile_g(k_norm_g[l]), seg, seq, tm)

        bexp, cexp, lam_tab = _ssm_params(ssm_lam_re[l], ssm_lam_im[l], ssm_log_step[l],
                                          ssm_b_re[l], ssm_b_im[l], ssm_c_re[l], ssm_c_im[l])
        y_ssm = _ssm(u, bexp, cexp, lam_tab, ssm_d[l][None, :].astype(F32),
                     ssm_w_glu[l].astype(BF16), ssm_norm_g[l][None, :].astype(F32),
                     bsz, seq, ts)

        lam = (jnp.exp(jnp.sum(lam_q1[l].astype(F32) * lam_k1[l].astype(F32)))
               - jnp.exp(jnp.sum(lam_q2[l].astype(F32) * lam_k2[l].astype(F32)))
               + lambda_init).reshape(1).astype(F32)
        y_att = _attention(lam, q, k, vt, subln_g[l], bsz, seq, tq, lambda_init)

        wr_t = w_router[l].astype(F32).T
        wr_hi = wr_t.astype(BF16)
        wr_lo = (wr_t - wr_hi.astype(F32)).astype(BF16)
        h1, xp, top_i, gates, rank, cnt = _out_router(
            h, y_ssm, y_att, w_out[l].astype(BF16), g_ffn[l][None, :].astype(F32),
            wr_hi, wr_lo, b_router[l].astype(F32)[:, None], tr)

        counts = cnt[:, 0].astype(I32)
        padded = (counts + rows - 1) // rows * rows
        pad_end = jnp.cumsum(padded)
        pad_start = pad_end - padded
        e_ids = jnp.arange(N_EXPERTS, dtype=I32)
        start_of = jnp.sum(jnp.where(top_i[None] == e_ids[:, None, None],
                                     pad_start[:, None, None], 0), axis=0)
        dest = start_of + rank
        dest3 = dest.reshape(TOP_K, n_tok // tr, tr).transpose(1, 0, 2).reshape(n_tok // tr, 1, TOP_K * tr)
        block_row = jnp.arange(n_blocks, dtype=I32) * rows
        block_expert = jnp.minimum(
            jnp.sum((pad_end[None, :] <= block_row[:, None]).astype(I32), axis=1),
            N_EXPERTS - 1).astype(I32)
        n_used = (pad_end[-1] // rows).reshape(1).astype(I32)
        spare = n_blocks - 1 - jnp.arange(N_EXPERTS, dtype=I32)
        zstart = jnp.concatenate([jnp.maximum(pad_end - rows, 0), spare * rows]).astype(I32)
        zflag = jnp.concatenate([padded > 0, spare >= n_used[0]]).astype(I32)

        xs = _dispatch(zstart, zflag, dest3, xp, n_rows, tr, rows)
        y_buf = _experts(block_expert, n_used, xs, w_up, b_up[:, :, None, :],
                         w_down, b_down[:, :, None, :], l, rows)
        h = _combine(dest3, h1, gates.T, y_buf, tr)
    return h.reshape(bsz, seq, D_MODEL).astype(x.dtype)
```
